```python
import jax, jax.numpy as jnp
from jax import lax
import numpy as np

D_MODEL = 1024
BATCH = 16
SEQ = 256
DEPTH = 2
DEC_BATCH = 8
DEC_SEQ = 4096
PAST_LEN = 512

GRID_W = 64
N_MIXERS = 2
EPS = 1e-6
N_HEADS = 16
N_KV_HEADS = 4
HEAD_DIM = D_MODEL // N_HEADS
ATTN_WIDTH = N_HEADS * HEAD_DIM
KV_WIDTH = N_KV_HEADS * HEAD_DIM
ATTN_IN = 2 * ATTN_WIDTH + 2 * KV_WIDTH
AXIS_DIM = HEAD_DIM // 2
ROPE_THETA = 10000.0
Q_BLOCK = 128
EXPAND = 2
D_INNER = EXPAND * D_MODEL
SSD_HEAD_DIM = 64
SSD_HEADS = D_INNER // SSD_HEAD_DIM
SSD_GROUPS = 4
D_STATE = 128
CONV_W = 3
CHUNK = 128
CONV_DIM = D_INNER + 2 * SSD_GROUPS * D_STATE
SSD_IN = D_INNER + CONV_DIM + 2 * SSD_HEADS

kernel_name = "hybrid_dit_attn_ssd_step"


def rms_norm(x, w):
    xf = x.astype(jnp.float32)
    y = xf * lax.rsqrt(jnp.mean(xf * xf, axis=-1, keepdims=True) + EPS)
    return (y * w.astype(jnp.float32)).astype(x.dtype)


def ada_mod(cond, mod_w, mod_b):
    m = jax.nn.silu(cond) @ mod_w + mod_b
    return jnp.split(m, 3, axis=-1)


def modulate(x, norm_w, shift, scale):
    return rms_norm(x, norm_w) * (1.0 + scale) + shift


def axial_rope_tables(n_tokens):
    rows = n_tokens // GRID_W
    row_ids = jnp.repeat(jnp.arange(rows), GRID_W).astype(jnp.float32)
    col_ids = jnp.tile(jnp.arange(GRID_W), rows).astype(jnp.float32)
    inv_freq = 1.0 / (ROPE_THETA ** (jnp.arange(0, AXIS_DIM, 2, dtype=jnp.float32) / AXIS_DIM))
    ang = jnp.concatenate([row_ids[:, None] * inv_freq, col_ids[:, None] * inv_freq], axis=-1)
    return jnp.cos(ang), jnp.sin(ang)


def apply_axial_rope(x, cos, sin):
    B, T, H, _ = x.shape
    half = AXIS_DIM // 2
    xr = x.astype(jnp.float32).reshape(B, T, H, 2, 2, half)
    a, b = xr[..., 0, :], xr[..., 1, :]
    cs = cos.reshape(T, 1, 2, half)
    sn = sin.reshape(T, 1, 2, half)
    out = jnp.stack([a * cs - b * sn, a * sn + b * cs], axis=-2)
    return out.reshape(x.shape).astype(x.dtype)


def attend(q, k, v):
    B, T = q.shape[:2]
    rep = N_HEADS // N_KV_HEADS
    qb = q.reshape(B, T // Q_BLOCK, Q_BLOCK, N_KV_HEADS, rep, HEAD_DIM).transpose(1, 0, 2, 3, 4, 5)
    scale = HEAD_DIM ** -0.5

    def block(qi):
        s = jnp.einsum('bqgrd,bsgd->bgrqs', qi, k, preferred_element_type=jnp.float32) * scale
        p = jax.nn.softmax(s, axis=-1).astype(v.dtype)
        return jnp.einsum('bgrqs,bsgd->bqgrd', p, v)

    o = lax.map(block, qb)
    return o.transpose(1, 0, 2, 3, 4, 5).reshape(B, T, ATTN_WIDTH)


def attn_qkvg(h, w_in, q_norm, k_norm):
    B, T, _ = h.shape
    proj = h @ w_in
    q, k, v, g = jnp.split(proj, [ATTN_WIDTH, ATTN_WIDTH + KV_WIDTH, ATTN_WIDTH + 2 * KV_WIDTH], axis=-1)
    q = rms_norm(q.reshape(B, T, N_HEADS, HEAD_DIM), q_norm)
    k = rms_norm(k.reshape(B, T, N_KV_HEADS, HEAD_DIM), k_norm)
    v = v.reshape(B, T, N_KV_HEADS, HEAD_DIM)
    return q, k, v, g


def attn_context(h, w_in, q_norm, k_norm, w_out):
    q, k, v, g = attn_qkvg(h, w_in, q_norm, k_norm)
    o = attend(q, k, v)
    return (o * jax.nn.silu(g)) @ w_out, k, v


def attn_latent(h, w_in, q_norm, k_norm, w_out, ctx_k, ctx_v, cos, sin):
    q, k, v, g = attn_qkvg(h, w_in, q_norm, k_norm)
    q = apply_axial_rope(q, cos, sin)
    k = apply_axial_rope(k, cos, sin)
    k_all = jnp.concatenate([ctx_k.astype(k.dtype), k], axis=1)
    v_all = jnp.concatenate([ctx_v.astype(v.dtype), v], axis=1)
    o = attend(q, k_all, v_all)
    return (o * jax.nn.silu(g)) @ w_out


def depthwise_conv_centred(x, w, b):
    out = lax.conv_general_dilated(
        x, w[:, None, :].astype(x.dtype), window_strides=(1,),
        padding=[(CONV_W // 2, CONV_W // 2)],
        dimension_numbers=('NWC', 'WIO', 'NWC'),
        feature_group_count=x.shape[-1])
    return out + b


def ssd_chunked(x, a, b, c, h0):
    Bs, T, H, P = x.shape
    G, N = b.shape[-2:]
    R = H // G
    nc = T // CHUNK
    x = x.reshape(Bs, nc, CHUNK, G, R, P)
    a = a.reshape(Bs, nc, CHUNK, G, R)
    b = b.reshape(Bs, nc, CHUNK, G, N)
    c = c.reshape(Bs, nc, CHUNK, G, N)
    a_cs = jnp.cumsum(a, axis=2)
    lower = jnp.tril(jnp.ones((CHUNK, CHUNK), bool))[:, :, None, None]
    decay = jnp.exp(jnp.where(lower, a_cs[:, :, :, None] - a_cs[:, :, None, :], -jnp.inf))
    cb = jnp.einsum('bclgn,bcsgn->bclsg', c, b)
    y_diag = jnp.einsum('bclsgr,bcsgrp->bclgrp', cb[..., None] * decay, x)
    decay_end = jnp.exp(a_cs[:, :, -1:] - a_cs)
    states = jnp.einsum('bclgn,bclgrp->bcgrpn', b, x * decay_end[..., None])
    chunk_decay = jnp.exp(a_cs[:, :, -1])

    def step(h, inp):
        s, d = inp
        return h * d[..., None, None] + s, h

    h_final, h_prev = lax.scan(step, h0.reshape(Bs, G, R, P, N),
                               (jnp.moveaxis(states, 1, 0), jnp.moveaxis(chunk_decay, 1, 0)))
    h_prev = jnp.moveaxis(h_prev, 0, 1)
    y_off = jnp.einsum('bclgn,bcgrpn->bclgrp', c, h_prev) * jnp.exp(a_cs)[..., None]
    return (y_diag + y_off).reshape(Bs, T, H, P), h_final.reshape(Bs, H, P, N)


def ssd_branch(h, w_in, conv_w, conv_b, dt_bias_f, dt_bias_b, a_log_f, a_log_b, d_skip, gnorm_w, w_out, h0_f, h0_b):
    f32 = jnp.float32
    B, T, _ = h.shape
    proj = h @ w_in
    z, xbc, dt = jnp.split(proj, [D_INNER, D_INNER + CONV_DIM], axis=-1)
    xbc = jax.nn.silu(depthwise_conv_centred(xbc, conv_w, conv_b))
    xs, bs, cs = jnp.split(xbc, [D_INNER, D_INNER + SSD_GROUPS * D_STATE], axis=-1)
    xs = xs.reshape(B, T, SSD_HEADS, SSD_HEAD_DIM).astype(f32)
    bs = bs.reshape(B, T, SSD_GROUPS, D_STATE).astype(f32)
    cs = cs.reshape(B, T, SSD_GROUPS, D_STATE).astype(f32)
    dt = jax.nn.softplus(dt.astype(f32) + jnp.concatenate([dt_bias_f, dt_bias_b]).astype(f32))
    dt_f, dt_b = dt[..., :SSD_HEADS], dt[..., SSD_HEADS:]
    a_f = -jnp.exp(a_log_f.astype(f32))
    a_b = -jnp.exp(a_log_b.astype(f32))
    y_f, h_f = ssd_chunked(xs * dt_f[..., None], a_f * dt_f, bs, cs, h0_f.astype(f32))
    flip = lambda t: jnp.flip(t, axis=1)
    y_b, h_b = ssd_chunked(flip(xs * dt_b[..., None]), flip(a_b * dt_b), flip(bs), flip(cs), h0_b.astype(f32))
    y = y_f + flip(y_b) + d_skip.astype(f32)[:, None] * xs
    y = y.reshape(B, T, D_INNER).astype(h.dtype)
    y = rms_norm(y * jax.nn.silu(z), gnorm_w)
    return y @ w_out, h_f.astype(h.dtype), h_b.astype(h.dtype)


def setup_inputs(seed: int = 0) -> dict:
    key = jax.random.key(seed)
    ks = iter(jax.random.split(key, 40))
    f32 = jnp.float32
    nrm = lambda shape, s: s * jax.random.normal(next(ks), shape, f32)
    gain = lambda n: 1.0 + 0.02 * jax.random.normal(next(ks), (n,), f32)

    def dt_bias():
        u = jax.random.uniform(next(ks), (SSD_HEADS,), f32)
        dt = jnp.exp(u * (np.log(0.1) - np.log(0.001)) + np.log(0.001))
        return dt + jnp.log(-jnp.expm1(-dt))

    def a_log():
        return jnp.log(jax.random.uniform(next(ks), (SSD_HEADS,), f32, 1.0, 16.0))

    inp = {}
    inp['x_prompt'] = nrm((BATCH, SEQ, D_MODEL), 1.0)
    inp['x_sample'] = nrm((DEC_BATCH, DEC_SEQ, D_MODEL), 1.0)
    inp['cache_k_l0'] = nrm((DEC_BATCH, PAST_LEN, N_KV_HEADS, HEAD_DIM), 1.0)
    inp['cache_v_l0'] = nrm((DEC_BATCH, PAST_LEN, N_KV_HEADS, HEAD_DIM), 1.0)
    inp['state_fwd_l1'] = nrm((DEC_BATCH, SSD_HEADS, SSD_HEAD_DIM, D_STATE), 0.1)
    inp['state_bwd_l1'] = nrm((DEC_BATCH, SSD_HEADS, SSD_HEAD_DIM, D_STATE), 0.1)
    inp['c'] = nrm((DEC_BATCH, D_MODEL), 1.0)
    inp['c_ctx'] = nrm((D_MODEL,), 1.0)
    inp['l0_norm_w'] = gain(D_MODEL)
    inp['l0_mod_w'] = nrm((D_MODEL, 3 * D_MODEL), 0.5 * D_MODEL ** -0.5)
    inp['l0_mod_b'] = nrm((3 * D_MODEL,), 0.02)
    inp['l0_w_in'] = nrm((D_MODEL, ATTN_IN), D_MODEL ** -0.5)
    inp['l0_q_norm'] = gain(HEAD_DIM)
    inp['l0_k_norm'] = gain(HEAD_DIM)
    inp['l0_w_out'] = nrm((ATTN_WIDTH, D_MODEL), ATTN_WIDTH ** -0.5)
    inp['l1_norm_w'] = gain(D_MODEL)
    inp['l1_mod_w'] = nrm((D_MODEL, 3 * D_MODEL), 0.5 * D_MODEL ** -0.5)
    inp['l1_mod_b'] = nrm((3 * D_MODEL,), 0.02)
    inp['l1_w_in'] = nrm((D_MODEL, SSD_IN), D_MODEL ** -0.5)
    inp['l1_conv_w'] = nrm((CONV_W, CONV_DIM), CONV_W ** -0.5)
    inp['l1_conv_b'] = nrm((CONV_DIM,), 0.02)
    inp['l1_dt_bias_f'] = dt_bias()
    inp['l1_dt_bias_b'] = dt_bias()
    inp['l1_a_log_f'] = a_log()
    inp['l1_a_log_b'] = a_log()
    inp['l1_d_skip'] = gain(SSD_HEADS)
    inp['l1_gnorm_w'] = gain(D_INNER)
    inp['l1_w_out'] = nrm((D_INNER, D_MODEL), D_INNER ** -0.5)
    return inp


def reference(x_prompt, x_sample, cache_k_l0, cache_v_l0, state_fwd_l1, state_bwd_l1, c, c_ctx,
              l0_norm_w, l0_mod_w, l0_mod_b, l0_w_in, l0_q_norm, l0_k_norm, l0_w_out,
              l1_norm_w, l1_mod_w, l1_mod_b, l1_w_in, l1_conv_w, l1_conv_b, l1_dt_bias_f, l1_dt_bias_b,
              l1_a_log_f, l1_a_log_b, l1_d_skip, l1_gnorm_w, l1_w_out):
    layers = [
        (l0_norm_w, l0_mod_w, l0_mod_b, (l0_w_in, l0_q_norm, l0_k_norm, l0_w_out)),
        (l1_norm_w, l1_mod_w, l1_mod_b, (l1_w_in, l1_conv_w, l1_conv_b, l1_dt_bias_f, l1_dt_bias_b,
                                         l1_a_log_f, l1_a_log_b, l1_d_skip, l1_gnorm_w, l1_w_out)),
    ]
    caches = [(cache_k_l0, cache_v_l0), (state_fwd_l1, state_bwd_l1)]
    cos, sin = axial_rope_tables(x_sample.shape[1])
    cond_ctx = c_ctx[None, None, :]
    cond_lat = c[:, None, :]
    xp, xs = x_prompt, x_sample
    ctx_out = []
    for i in range(DEPTH):
        norm_w, mod_w, mod_b, mp = layers[i]
        cache_a, cache_b = caches[i]
        sh_p, sc_p, g_p = ada_mod(cond_ctx, mod_w, mod_b)
        sh_s, sc_s, g_s = ada_mod(cond_lat, mod_w, mod_b)
        hp = modulate(xp, norm_w, sh_p, sc_p)
        hs = modulate(xs, norm_w, sh_s, sc_s)
        if i % N_MIXERS == 0:
            yp, k_new, v_new = attn_context(hp, *mp)
            ys = attn_latent(hs, *mp, cache_a, cache_b, cos, sin)
            ctx_out.append((k_new, v_new))
        else:
            zeros = jnp.zeros((xp.shape[0], SSD_HEADS, SSD_HEAD_DIM, D_STATE), xp.dtype)
            yp, hf_new, hb_new = ssd_branch(hp, *mp, zeros, zeros)
            ys, _, _ = ssd_branch(hs, *mp, cache_a, cache_b)
            ctx_out.append((hf_new, hb_new))
        xp = xp + g_p * yp
        xs = xs + g_s * ys
    (new_k_l0, new_v_l0), (new_state_fwd_l1, new_state_bwd_l1) = ctx_out
    return (xp, xs, new_k_l0, new_v_l0, new_state_fwd_l1, new_state_bwd_l1)
```

```python
import functools
import math

import jax
import jax.numpy as jnp
from jax import lax
from jax.experimental import pallas as pl
from jax.experimental.pallas import tpu as pltpu

f32 = jnp.float32
bf16 = jnp.bfloat16

D_MODEL = 1024
GRID_W = 64
EPS = 1e-6
N_HEADS = 16
N_KV_HEADS = 4
HEAD_DIM = 64
ATTN_WIDTH = N_HEADS * HEAD_DIM
KV_WIDTH = N_KV_HEADS * HEAD_DIM
AXIS_DIM = HEAD_DIM // 2
ROPE_THETA = 10000.0
D_INNER = 2048
SSD_HEAD_DIM = 64
SSD_HEADS = D_INNER // SSD_HEAD_DIM
SSD_GROUPS = 4
HEADS_PER_GROUP = SSD_HEADS // SSD_GROUPS
D_STATE = 128
CHUNK = 128
BC_WIDTH = 2 * SSD_GROUPS * D_STATE
CONV_DIM = D_INNER + BC_WIDTH

LANES = 128
HALO = 8
VMEM_LIMIT = 56 * 1024 * 1024

NEG_BIG = -1e30


def _params(n_axes):
    return pltpu.CompilerParams(
        dimension_semantics=("arbitrary",) * n_axes, vmem_limit_bytes=VMEM_LIMIT)


def _dot(a, b):
    return jnp.dot(a, b, preferred_element_type=f32)


def _dot_nt(a, b):
    return lax.dot_general(a, b, (((1,), (1,)), ((), ())), preferred_element_type=f32)


def _split(x):
    hi = x.astype(bf16)
    lo = (x - hi.astype(f32)).astype(bf16)
    return hi, lo


def _silu(x):
    return x * jax.nn.sigmoid(x)


def _softplus(x):
    return jnp.maximum(x, 0.0) + jnp.log1p(jnp.exp(-jnp.abs(x)))


def _modulated_norm(x, norm_w, shift, scale):
    ms = jnp.mean(x * x, axis=-1, keepdims=True)
    return (x * lax.rsqrt(ms + EPS) * norm_w) * (1.0 + scale) + shift


def _ada_mod_kernel(cond_ref, w_ref, b_ref, o_ref):
    s = _silu(cond_ref[...])
    sh, sl = _split(s)
    wh, wl = _split(w_ref[...])
    o_ref[...] = _dot(sh, wh) + (_dot(sh, wl) + _dot(sl, wh)) + b_ref[...]


def _ada_mod(cond, mod_w, mod_b):
    rows = cond.shape[0]
    tn = 1024
    return pl.pallas_call(
        _ada_mod_kernel,
        grid=(3 * D_MODEL // tn,),
        in_specs=[pl.BlockSpec((rows, D_MODEL), lambda j: (0, 0)),
                  pl.BlockSpec((D_MODEL, tn), lambda j: (0, j)),
                  pl.BlockSpec((1, tn), lambda j: (0, j))],
        out_specs=pl.BlockSpec((rows, tn), lambda j: (0, j)),
        out_shape=jax.ShapeDtypeStruct((rows, 3 * D_MODEL), f32),
        compiler_params=_params(1),
        name="ada_mod",
    )(cond, mod_w, mod_b.reshape(1, -1))


def _head_rms(t, e, et, w):
    hi, lo = _split(t * t)
    ss = _dot(hi, e) + _dot(lo, e)
    rh, rl = _split(lax.rsqrt(ss * (1.0 / HEAD_DIM) + EPS))
    return t * (_dot(rh, et) + _dot(rl, et)) * w


def _rope(t, cos, sin_a, sin_b):
    width = t.shape[1]
    rep = width // LANES
    tile = lambda u: jnp.concatenate([u] * rep, axis=1)
    return (t * tile(cos) + pltpu.roll(t, width - AXIS_DIM // 2, 1) * tile(sin_a)
            + pltpu.roll(t, AXIS_DIM // 2, 1) * tile(sin_b))


def _l0_inproj_kernel(*refs, latent):
    x_ref, mod_ref, nw_ref, w_ref, qn_ref, kn_ref, e_ref, et_ref = refs[:8]
    refs = refs[8:]
    if latent:
        cos_ref, sa_ref, sb_ref = refs[:3]
        q_ref, kcat_ref, vcat_ref, gs_ref = refs[3:]
    else:
        q_ref, kcat_ref, vcat_ref, gs_ref, knew_ref, vnew_ref = refs
    mod = mod_ref[0]
    h = _modulated_norm(x_ref[0], nw_ref[...], mod[:, :D_MODEL], mod[:, D_MODEL:2 * D_MODEL])
    proj = _dot(h.astype(bf16), w_ref[...])
    q = proj[:, :ATTN_WIDTH]
    k = proj[:, ATTN_WIDTH:ATTN_WIDTH + KV_WIDTH]
    v = proj[:, ATTN_WIDTH + KV_WIDTH:ATTN_WIDTH + 2 * KV_WIDTH]
    g = proj[:, ATTN_WIDTH + 2 * KV_WIDTH:]
    e = e_ref[...]
    et = et_ref[...]
    q = _head_rms(q, e, et, qn_ref[...])
    k = _head_rms(k, e[:KV_WIDTH], et[:, :KV_WIDTH], kn_ref[...])
    if latent:
        cos, sa, sb = cos_ref[...], sa_ref[...], sb_ref[...]
        q = _rope(q, cos, sa, sb)
        k = _rope(k, cos, sa, sb)
    else:
        knew_ref[0] = k
        vnew_ref[0] = v
    q_ref[0] = q.astype(bf16)
    gs_ref[0] = _silu(g).astype(bf16)
    rows = k.shape[0]
    low = lax.broadcasted_iota(jnp.int32, (rows, LANES), 1) < HEAD_DIM
    for j in range(KV_WIDTH // LANES):
        kb = k[:, LANES * j:LANES * (j + 1)]
        vb = v[:, LANES * j:LANES * (j + 1)]
        kbs = pltpu.roll(kb, HEAD_DIM, 1)
        vbs = pltpu.roll(vb, HEAD_DIM, 1)
        for par, (k_lo, k_hi, v_lo, v_hi) in enumerate(((kb, kbs, vb, vbs), (kbs, kb, vbs, vb))):
            kvh = 2 * j + par
            kcat_ref[0, kvh, :, :LANES] = jnp.where(low, k_lo, 0.0).astype(bf16)
            kcat_ref[0, kvh, :, LANES:] = jnp.where(low, 0.0, k_hi).astype(bf16)
            vcat_ref[0, kvh, :, :LANES] = jnp.where(low, v_lo, 1.0).astype(bf16)
            vcat_ref[0, kvh, :, LANES:] = jnp.where(low, 1.0, v_hi).astype(bf16)


def _l0_inproj(x, mod, norm_w, w_in, qn, kn, e, et, rope_tabs, *, latent, tm):
    b, t, _ = x.shape
    attn_in = w_in.shape[1]
    row = (lambda bi, i: (bi + 1, 0, 0)) if latent else (lambda bi, i: (0, 0, 0))
    const = lambda bi, i: (0, 0)
    in_specs = [pl.BlockSpec((1, tm, D_MODEL), lambda bi, i: (bi, i, 0)),
                pl.BlockSpec((1, 1, 3 * D_MODEL), row),
                pl.BlockSpec((1, D_MODEL), const),
                pl.BlockSpec((D_MODEL, attn_in), const),
                pl.BlockSpec((1, ATTN_WIDTH), const),
                pl.BlockSpec((1, KV_WIDTH), const),
                pl.BlockSpec((ATTN_WIDTH, LANES), const),
                pl.BlockSpec((LANES, ATTN_WIDTH), const)]
    args = [x, mod, norm_w, w_in, qn, kn, e, et]
    tok = lambda bi, i: (bi, i, 0)
    out_specs = [pl.BlockSpec((1, tm, ATTN_WIDTH), tok),
                 pl.BlockSpec((1, N_KV_HEADS, tm, 2 * LANES), lambda bi, i: (bi, 0, i, 0)),
                 pl.BlockSpec((1, N_KV_HEADS, tm, 2 * LANES), lambda bi, i: (bi, 0, i, 0)),
                 pl.BlockSpec((1, tm, ATTN_WIDTH), tok)]
    out_shape = [jax.ShapeDtypeStruct((b, t, ATTN_WIDTH), bf16),
                 jax.ShapeDtypeStruct((b, N_KV_HEADS, t, 2 * LANES), bf16),
                 jax.ShapeDtypeStruct((b, N_KV_HEADS, t, 2 * LANES), bf16),
                 jax.ShapeDtypeStruct((b, t, ATTN_WIDTH), bf16)]
    if latent:
        in_specs += [pl.BlockSpec((tm, LANES), lambda bi, i: (i, 0))] * 3
        args += list(rope_tabs)
    else:
        out_specs += [pl.BlockSpec((1, tm, KV_WIDTH), tok)] * 2
        out_shape += [jax.ShapeDtypeStruct((b, t, KV_WIDTH), f32)] * 2
    return pl.pallas_call(
        functools.partial(_l0_inproj_kernel, latent=latent),
        grid=(b, t // tm),
        in_specs=in_specs, out_specs=out_specs, out_shape=out_shape,
        compiler_params=_params(2),
        name="l0_inproj_latent" if latent else "l0_inproj_context",
    )(*args)


def _attn_kernel(*refs, n_src):
    q_ref = refs[0]
    kv_refs = refs[1:1 + 2 * n_src]
    o_ref = refs[-1]
    tq = q_ref.shape[1]
    low = lax.broadcasted_iota(jnp.int32, (tq, LANES), 1) < HEAD_DIM
    for pair in range(2):
        qp = q_ref[0, :, LANES * pair:LANES * (pair + 1)]
        outs = []
        for par in range(2):
            cols = slice(LANES * par, LANES * (par + 1))
            scores = [_dot_nt(qp, kv_refs[2 * i][0, 0, :, cols]) for i in range(n_src)]
            m = functools.reduce(jnp.maximum,
                                 [jnp.max(s, axis=-1, keepdims=True) for s in scores])
            acc = None
            for i, s in enumerate(scores):
                part = _dot(jnp.exp2(s - m).astype(bf16), kv_refs[2 * i + 1][0, 0, :, cols])
                acc = part if acc is None else acc + part
            outs.append(acc / pltpu.roll(acc, HEAD_DIM, 1))
        o_ref[0, :, LANES * pair:LANES * (pair + 1)] = jnp.where(low, outs[0], outs[1]).astype(bf16)


def _attention(q, kv_sources, *, tq):
    b, t, _ = q.shape
    in_specs = [pl.BlockSpec((1, tq, 2 * LANES), lambda bi, g, i: (bi, i, g))]
    args = [q]
    for arr in kv_sources:
        s = arr.shape[2]
        in_specs.append(pl.BlockSpec((1, 1, s, 2 * LANES), lambda bi, g, i: (bi, g, 0, 0)))
        args.append(arr)
    return pl.pallas_call(
        functools.partial(_attn_kernel, n_src=len(kv_sources) // 2),
        grid=(b, N_KV_HEADS, t // tq),
        in_specs=in_specs,
        out_specs=pl.BlockSpec((1, tq, 2 * LANES), lambda bi, g, i: (bi, i, g)),
        out_shape=jax.ShapeDtypeStruct((b, t, ATTN_WIDTH), bf16),
        compiler_params=_params(3),
        name="attention_latent" if len(kv_sources) > 2 else "attention_context",
    )(*args)


def _l0_outproj_kernel(x_ref, o_ref, gs_ref, mod_ref, w_ref, y_ref):
    gate = mod_ref[0][:, 2 * D_MODEL:]
    y_ref[0] = x_ref[0] + gate * _dot(o_ref[0] * gs_ref[0], w_ref[...])


def _l0_outproj(x, o, gs, mod, w_out, *, latent, tm):
    b, t, _ = x.shape
    row = (lambda bi, i: (bi + 1, 0, 0)) if latent else (lambda bi, i: (0, 0, 0))
    tok = lambda bi, i: (bi, i, 0)
    return pl.pallas_call(
        _l0_outproj_kernel,
        grid=(b, t // tm),
        in_specs=[pl.BlockSpec((1, tm, D_MODEL), tok),
                  pl.BlockSpec((1, tm, ATTN_WIDTH), tok),
                  pl.BlockSpec((1, tm, ATTN_WIDTH), tok),
                  pl.BlockSpec((1, 1, 3 * D_MODEL), row),
                  pl.BlockSpec((ATTN_WIDTH, D_MODEL), lambda bi, i: (0, 0))],
        out_specs=pl.BlockSpec((1, tm, D_MODEL), tok),
        out_shape=jax.ShapeDtypeStruct((b, t, D_MODEL), f32),
        compiler_params=_params(2),
        name="l0_outproj",
    )(x, o, gs, mod, w_out)


def _l1_inproj_kernel(x_ref, xp_ref, xn_ref, mod_ref, nw_ref, wzx_ref, wdt_ref, cw_ref, cb_ref,
                      dtb_ref, z_ref, xs_ref, bc_ref, dt_ref):
    i = pl.program_id(1)
    last = pl.num_programs(1) - 1
    tm = x_ref.shape[1]
    mod = mod_ref[0]
    xe = jnp.concatenate([xp_ref[0], x_ref[0], xn_ref[0]], axis=0)
    h = _modulated_norm(xe, nw_ref[...], mod[:, :D_MODEL], mod[:, D_MODEL:2 * D_MODEL]).astype(bf16)
    proj = _dot(h, wzx_ref[...])
    z_ref[0] = proj[HALO:HALO + tm, :D_INNER].astype(bf16)
    xbc = proj[:, D_INNER:]
    rows = xbc.shape[0]
    r = lax.broadcasted_iota(jnp.int32, (tm, 1), 0)
    cur = xbc[HALO:HALO + tm]
    prev = pltpu.roll(xbc, 1, 0)[HALO:HALO + tm]
    nxt = pltpu.roll(xbc, rows - 1, 0)[HALO:HALO + tm]
    prev = jnp.where(r == jnp.where(i == 0, 0, -1), 0.0, prev)
    nxt = jnp.where(r == jnp.where(i == last, tm - 1, -1), 0.0, nxt)
    cw = cw_ref[...]
    act = _silu(cw[0:1] * prev + cw[1:2] * cur + cw[2:3] * nxt + cb_ref[...])
    xs_ref[0] = act[:, :D_INNER].astype(bf16)
    bc_ref[0] = act[:, D_INNER:].astype(bf16)
    dt_ref[0] = _softplus(_dot(h[HALO:HALO + tm], wdt_ref[...]) + dtb_ref[...])


def _l1_inproj(x1, mod, norm_w, w_zx, w_dt, conv_w, conv_b, dt_bias, *, latent, tm):
    b, t, _ = x1.shape
    row = (lambda bi, i: (bi + 1, 0, 0)) if latent else (lambda bi, i: (0, 0, 0))
    const = lambda bi, i: (0, 0)
    tok = lambda bi, i: (bi, i, 0)
    hb = tm // HALO
    n_hb = t // HALO
    prev_map = lambda bi, i: (bi, jnp.maximum(i * hb - 1, 0), 0)
    next_map = lambda bi, i: (bi, jnp.minimum((i + 1) * hb, n_hb - 1), 0)
    return pl.pallas_call(
        _l1_inproj_kernel,
        grid=(b, t // tm),
        in_specs=[pl.BlockSpec((1, tm, D_MODEL), tok),
                  pl.BlockSpec((1, HALO, D_MODEL), prev_map),
                  pl.BlockSpec((1, HALO, D_MODEL), next_map),
                  pl.BlockSpec((1, 1, 3 * D_MODEL), row),
                  pl.BlockSpec((1, D_MODEL), const),
                  pl.BlockSpec((D_MODEL, D_INNER + CONV_DIM), const),
                  pl.BlockSpec((D_MODEL, LANES), const),
                  pl.BlockSpec((3, CONV_DIM), const),
                  pl.BlockSpec((1, CONV_DIM), const),
                  pl.BlockSpec((1, LANES), const)],
        out_specs=[pl.BlockSpec((1, tm, D_INNER), tok),
                   pl.BlockSpec((1, tm, D_INNER), tok),
                   pl.BlockSpec((1, tm, BC_WIDTH), tok),
                   pl.BlockSpec((1, tm, LANES), tok)],
        out_shape=[jax.ShapeDtypeStruct((b, t, D_INNER), bf16),
                   jax.ShapeDtypeStruct((b, t, D_INNER), bf16),
                   jax.ShapeDtypeStruct((b, t, BC_WIDTH), bf16),
                   jax.ShapeDtypeStruct((b, t, LANES), f32)],
        compiler_params=_params(2),
        name="l1_inproj",
    )(x1, x1, x1, mod, norm_w, w_zx, w_dt, conv_w, conv_b, dt_bias)


def _ssd_kernel(*refs, backward, has_h0, final_state):
    xs_ref, bc_ref, dt_ref, alog_ref, lmat_ref, umat_ref = refs[:6]
    refs = refs[6:]
    if has_h0:
        h0_ref, refs = refs[0], refs[1:]
    if backward:
        yf_ref, z_ref, x1_ref, mod_ref, gw_ref, wout_ref, dskip_ref = refs[:7]
        refs = refs[7:]
    out_ref, refs = refs[0], refs[1:]
    if final_state:
        hfin_ref, refs = refs[0], refs[1:]
    ht_ref = refs[0]
    if backward:
        yb_ref = refs[1]

    c = pl.program_id(1)
    n_pairs = SSD_HEADS // 2

    @pl.when(c == 0)
    def _init():
        if has_h0:
            for j in range(n_pairs):
                ht_ref[:, LANES * j:LANES * (j + 1)] = h0_ref[0, LANES * j:LANES * (j + 1), :].T
        else:
            ht_ref[...] = jnp.zeros_like(ht_ref)

    xs = xs_ref[0]
    dt = dt_ref[0]
    adt = dt * (-jnp.exp(alog_ref[...]))
    adt_t = adt.T
    dt_t = dt.T
    lower = lmat_ref[...]
    upper = umat_ref[...]
    ah, al = _split(adt)
    th, tl = _split(adt_t)
    if backward:
        acs = _dot(upper, ah) + _dot(upper, al)
        acs_t = _dot(th, lower) + _dot(tl, lower)
        end = 0
    else:
        acs = _dot(lower, ah) + _dot(lower, al)
        acs_t = _dot(th, upper) + _dot(tl, upper)
        end = CHUNK - 1
    eacs = jnp.exp(acs)
    chunk_decay = eacs[end:end + 1, :]
    w_t = dt_t * jnp.exp(acs_t[:, end:end + 1] - acs_t)
    ri = lax.broadcasted_iota(jnp.int32, (CHUNK, CHUNK), 0)
    ci = lax.broadcasted_iota(jnp.int32, (CHUNK, CHUNK), 1)
    mask = (ri <= ci) if backward else (ri >= ci)
    low = lax.broadcasted_iota(jnp.int32, (CHUNK, LANES), 1) < SSD_HEAD_DIM
    low_row = low[0:1]
    h_off = SSD_HEADS if backward else 0
    y_dst = yb_ref if backward else out_ref.at[0]

    for g in range(SSD_GROUPS):
        bm = bc_ref[0, :, D_STATE * g:D_STATE * (g + 1)]
        cm = bc_ref[0, :, D_STATE * (SSD_GROUPS + g):D_STATE * (SSD_GROUPS + g + 1)]
        cb = _dot_nt(cm, bm)
        bm_t = bm.astype(f32).T
        cm_f = cm.astype(f32)
        for jp in range(HEADS_PER_GROUP // 2):
            j = g * (HEADS_PER_GROUP // 2) + jp
            cols = slice(LANES * j, LANES * (j + 1))
            xp = xs[:, cols]
            rhs = jnp.concatenate([xp, ht_ref[:, cols].astype(bf16)], axis=0)
            ys, ss, cds = [], [], []
            for par in range(2):
                hh = h_off + 2 * j + par
                decay = jnp.exp(jnp.where(mask, acs[:, hh:hh + 1] - acs_t[hh:hh + 1, :], NEG_BIG))
                m_diag = (cb * decay * dt_t[hh:hh + 1, :]).astype(bf16)
                c_off = (cm_f * eacs[:, hh:hh + 1]).astype(bf16)
                ys.append(_dot(jnp.concatenate([m_diag, c_off], axis=1), rhs))
                ss.append(_dot((bm_t * w_t[hh:hh + 1, :]).astype(bf16), xp))
                cds.append(chunk_decay[:, hh:hh + 1])
            y_dst[:, cols] = jnp.where(low, ys[0], ys[1])
            ht_ref[:, cols] = (ht_ref[:, cols] * jnp.where(low_row, cds[0], cds[1])
                               + jnp.where(low, ss[0], ss[1]))

    if backward:
        y = yf_ref[0] + yb_ref[...] + dskip_ref[...] * xs.astype(f32)
        yz = y * _silu(z_ref[0].astype(f32))
        ms = jnp.mean(yz * yz, axis=-1, keepdims=True)
        yn = (yz * lax.rsqrt(ms + EPS) * gw_ref[...]).astype(bf16)
        gate = mod_ref[0][:, 2 * D_MODEL:]
        out_ref[0] = x1_ref[0] + gate * _dot(yn, wout_ref[...])

    if final_state:
        @pl.when(c == pl.num_programs(1) - 1)
        def _fin():
            for j in range(n_pairs):
                hfin_ref[0, LANES * j:LANES * (j + 1), :] = ht_ref[:, LANES * j:LANES * (j + 1)].T


def _ssd_pass(xs, bc, dt, alog, lmat, umat, h0, extra, *, backward, latent):
    b, t, _ = xs.shape
    nc = t // CHUNK
    has_h0 = h0 is not None
    final_state = not latent
    chunk = (lambda bi, c: (bi, nc - 1 - c, 0)) if backward else (lambda bi, c: (bi, c, 0))
    const = lambda bi, c: (0, 0)
    per_b = lambda bi, c: (bi, 0, 0)
    row = (lambda bi, c: (bi + 1, 0, 0)) if latent else (lambda bi, c: (0, 0, 0))
    in_specs = [pl.BlockSpec((1, CHUNK, D_INNER), chunk),
                pl.BlockSpec((1, CHUNK, BC_WIDTH), chunk),
                pl.BlockSpec((1, CHUNK, LANES), chunk),
                pl.BlockSpec((1, LANES), const),
                pl.BlockSpec((CHUNK, CHUNK), const),
                pl.BlockSpec((CHUNK, CHUNK), const)]
    args = [xs, bc, dt, alog, lmat, umat]
    if has_h0:
        in_specs.append(pl.BlockSpec((1, D_INNER, D_STATE), per_b))
        args.append(h0)
    scratch = [pltpu.VMEM((D_STATE, D_INNER), f32)]
    if backward:
        yf, z, x1, mod, gw, wout, dskip = extra
        in_specs += [pl.BlockSpec((1, CHUNK, D_INNER), chunk),
                     pl.BlockSpec((1, CHUNK, D_INNER), chunk),
                     pl.BlockSpec((1, CHUNK, D_MODEL), chunk),
                     pl.BlockSpec((1, 1, 3 * D_MODEL), row),
                     pl.BlockSpec((1, D_INNER), const),
                     pl.BlockSpec((D_INNER, D_MODEL), const),
                     pl.BlockSpec((1, D_INNER), const)]
        args += [yf, z, x1, mod, gw, wout, dskip]
        out_specs = [pl.BlockSpec((1, CHUNK, D_MODEL), chunk)]
        out_shape = [jax.ShapeDtypeStruct((b, t, D_MODEL), f32)]
        scratch.append(pltpu.VMEM((CHUNK, D_INNER), f32))
    else:
        out_specs = [pl.BlockSpec((1, CHUNK, D_INNER), chunk)]
        out_shape = [jax.ShapeDtypeStruct((b, t, D_INNER), f32)]
    if final_state:
        out_specs.append(pl.BlockSpec((1, D_INNER, D_STATE), per_b))
        out_shape.append(jax.ShapeDtypeStruct((b, D_INNER, D_STATE), f32))
    return pl.pallas_call(
        functools.partial(_ssd_kernel, backward=backward, has_h0=has_h0, final_state=final_state),
        grid=(b, nc),
        in_specs=in_specs, out_specs=out_specs, out_shape=out_shape,
        scratch_shapes=scratch,
        compiler_params=_params(2),
        name="ssd_bwd" if backward else "ssd_fwd",
    )(*args)


def _rope_tables(n_tokens):
    rows = n_tokens // GRID_W
    row_ids = jnp.repeat(jnp.arange(rows), GRID_W).astype(f32)
    col_ids = jnp.tile(jnp.arange(GRID_W), rows).astype(f32)
    inv_freq = 1.0 / (ROPE_THETA ** (jnp.arange(0, AXIS_DIM, 2, dtype=f32) / AXIS_DIM))
    ang = jnp.stack([row_ids[:, None] * inv_freq, col_ids[:, None] * inv_freq], axis=1)
    cos, sin, zero = jnp.cos(ang), jnp.sin(ang), jnp.zeros_like(ang)
    head = lambda a, b_: jnp.stack([a, b_], axis=2).reshape(n_tokens, HEAD_DIM)
    two = lambda u: jnp.concatenate([u, u], axis=1)
    return two(head(cos, cos)), two(head(-sin, zero)), two(head(zero, sin))


def kernel(x_prompt, x_sample, cache_k_l0, cache_v_l0, state_fwd_l1, state_bwd_l1, c, c_ctx,
           l0_norm_w, l0_mod_w, l0_mod_b, l0_w_in, l0_q_norm, l0_k_norm, l0_w_out,
           l1_norm_w, l1_mod_w, l1_mod_b, l1_w_in, l1_conv_w, l1_conv_b, l1_dt_bias_f, l1_dt_bias_b,
           l1_a_log_f, l1_a_log_b, l1_d_skip, l1_gnorm_w, l1_w_out):
    dec_b = x_sample.shape[0]
    pad = lambda u, n: jnp.pad(u, (0, n - u.shape[0]))

    n_rows = 16
    cond = jnp.zeros((n_rows, D_MODEL), f32).at[0].set(c_ctx).at[1:1 + dec_b].set(c)
    mod0 = _ada_mod(cond, l0_mod_w, l0_mod_b).reshape(n_rows, 1, 3 * D_MODEL)
    mod1 = _ada_mod(cond, l1_mod_w, l1_mod_b).reshape(n_rows, 1, 3 * D_MODEL)

    q_scale = HEAD_DIM ** -0.5 * math.log2(math.e)
    qn = (jnp.tile(l0_q_norm, N_HEADS) * q_scale).reshape(1, ATTN_WIDTH)
    kn = jnp.tile(l0_k_norm, N_KV_HEADS).reshape(1, KV_WIDTH)
    head_of = jnp.arange(ATTN_WIDTH) // HEAD_DIM
    e = (head_of[:, None] == jnp.arange(LANES)[None, :]).astype(bf16)
    et = e.T
    w_in0 = l0_w_in.astype(bf16)
    w_out0 = l0_w_out.astype(bf16)
    w_zx = l1_w_in[:, :D_INNER + CONV_DIM].astype(bf16)
    w_dt = jnp.pad(l1_w_in[:, D_INNER + CONV_DIM:], ((0, 0), (0, LANES - 2 * SSD_HEADS))).astype(bf16)
    w_out1 = l1_w_out.astype(bf16)
    dt_bias = pad(jnp.concatenate([l1_dt_bias_f, l1_dt_bias_b]), LANES).reshape(1, LANES)
    alog = pad(jnp.concatenate([l1_a_log_f, l1_a_log_b]), LANES).reshape(1, LANES)
    dskip = jnp.repeat(l1_d_skip, SSD_HEAD_DIM).reshape(1, D_INNER)
    nw0 = l0_norm_w.reshape(1, D_MODEL)
    nw1 = l1_norm_w.reshape(1, D_MODEL)
    gw = l1_gnorm_w.reshape(1, D_INNER)
    conv_b = l1_conv_b.reshape(1, CONV_DIM)
    tri = jnp.arange(CHUNK)
    lmat = (tri[None, :] <= tri[:, None]).astype(bf16)
    umat = lmat.T
    rope_tabs = _rope_tables(x_sample.shape[1])

    kc = cache_k_l0.transpose(0, 2, 1, 3).astype(bf16)
    vc = cache_v_l0.transpose(0, 2, 1, 3).astype(bf16)
    zero, one = jnp.zeros_like(kc), jnp.ones_like(vc)
    kcat_ctx = jnp.concatenate([kc, zero, zero, kc], axis=-1)
    vcat_ctx = jnp.concatenate([vc, one, one, vc], axis=-1)
    h0_f = state_fwd_l1.reshape(dec_b, D_INNER, D_STATE)
    h0_b = state_bwd_l1.reshape(dec_b, D_INNER, D_STATE)

    outs = {}
    for latent, x in ((False, x_prompt), (True, x_sample)):
        tm = 256
        res = _l0_inproj(x, mod0, nw0, w_in0, qn, kn, e, et, rope_tabs, latent=latent, tm=tm)
        if latent:
            q, kcat, vcat, gs = res
            kv = [kcat_ctx, vcat_ctx, kcat, vcat]
        else:
            q, kcat, vcat, gs, k_new, v_new = res
            kv = [kcat, vcat]
        o = _attention(q, kv, tq=256)
        x1 = _l0_outproj(x, o, gs, mod0, w_out0, latent=latent, tm=tm)
        z, xs, bc, dt = _l1_inproj(x1, mod1, nw1, w_zx, w_dt, l1_conv_w, conv_b, dt_bias,
                                   latent=latent, tm=tm)
        fwd = _ssd_pass(xs, bc, dt, alog, lmat, umat, h0_f if latent else None, None,
                        backward=False, latent=latent)
        bwd = _ssd_pass(xs, bc, dt, alog, lmat, umat, h0_b if latent else None,
                        (fwd[0], z, x1, mod1, gw, w_out1, dskip), backward=True, latent=latent)
        if latent:
            outs["y_sample"] = bwd[0]
        else:
            b = x.shape[0]
            outs["y_prompt"] = bwd[0]
            outs["k"] = k_new.reshape(b, -1, N_KV_HEADS, HEAD_DIM)
            outs["v"] = v_new.reshape(b, -1, N_KV_HEADS, HEAD_DIM)
            outs["hf"] = fwd[1].reshape(b, SSD_HEADS, SSD_HEAD_DIM, D_STATE)
            outs["hb"] = bwd[1].reshape(b, SSD_HEADS, SSD_HEAD_DIM, D_STATE)
    return (outs["y_prompt"], outs["y_sample"], outs["k"], outs["v"], outs["hf"], outs["hb"])
```

```python
import functools
import math

import jax
import jax.numpy as jnp
from jax import lax
from jax.experimental import pallas as pl
from jax.experimental.pallas import tpu as pltpu

f32 = jnp.float32
bf16 = jnp.bfloat16

D_MODEL = 1024
GRID_W = 64
EPS = 1e-6
N_HEADS = 16
N_KV_HEADS = 4
HEAD_DIM = 64
ATTN_WIDTH = N_HEADS * HEAD_DIM
KV_WIDTH = N_KV_HEADS * HEAD_DIM
AXIS_DIM = HEAD_DIM // 2
ROPE_THETA = 10000.0
D_INNER = 2048
SSD_HEAD_DIM = 64
SSD_HEADS = D_INNER // SSD_HEAD_DIM
SSD_GROUPS = 4
HEADS_PER_GROUP = SSD_HEADS // SSD_GROUPS
D_STATE = 128
CHUNK = 128
BC_WIDTH = 2 * SSD_GROUPS * D_STATE
CONV_DIM = D_INNER + BC_WIDTH

LANES = 128
HALO = 8
BF16_ROWS = 16
VT_ROWS = HEAD_DIM + BF16_ROWS
KEY_CHUNK = 256
SCORE_LOOKAHEAD = 8
VMEM_LIMIT = 56 * 1024 * 1024

NEG_BIG = -1e30


def _params(n_axes):
    return pltpu.CompilerParams(
        dimension_semantics=("arbitrary",) * n_axes, vmem_limit_bytes=VMEM_LIMIT)


def _dot(a, b):
    return jnp.dot(a, b, preferred_element_type=f32)


def _dot_nt(a, b):
    return lax.dot_general(a, b, (((1,), (1,)), ((), ())), preferred_element_type=f32)


def _split(x):
    hi = x.astype(bf16)
    lo = (x - hi.astype(f32)).astype(bf16)
    return hi, lo


def _silu(x):
    return x * jax.nn.sigmoid(x)


def _softplus(x):
    return jnp.maximum(x, 0.0) + jnp.log1p(jnp.exp(-jnp.abs(x)))


def _modulated_norm(x, norm_w, shift, scale):
    ms = jnp.mean(x * x, axis=-1, keepdims=True)
    return (x * lax.rsqrt(ms + EPS) * norm_w) * (1.0 + scale) + shift


def _ada_mod_kernel(cond_ref, w_ref, b_ref, o_ref):
    s = _silu(cond_ref[...])
    sh, sl = _split(s)
    wh, wl = _split(w_ref[...])
    o_ref[...] = _dot(sh, wh) + (_dot(sh, wl) + _dot(sl, wh)) + b_ref[...]


def _ada_mod(cond, mod_w, mod_b):
    rows = cond.shape[0]
    tn = 1024
    return pl.pallas_call(
        _ada_mod_kernel,
        grid=(3 * D_MODEL // tn,),
        in_specs=[pl.BlockSpec((rows, D_MODEL), lambda j: (0, 0)),
                  pl.BlockSpec((D_MODEL, tn), lambda j: (0, j)),
                  pl.BlockSpec((1, tn), lambda j: (0, j))],
        out_specs=pl.BlockSpec((rows, tn), lambda j: (0, j)),
        out_shape=jax.ShapeDtypeStruct((rows, 3 * D_MODEL), f32),
        compiler_params=_params(1),
        name="ada_mod",
    )(cond, mod_w, mod_b.reshape(1, -1))


def _head_rms(t, e, et, w):
    hi, lo = _split(t * t)
    ss = _dot(hi, e) + _dot(lo, e)
    rh, rl = _split(lax.rsqrt(ss * (1.0 / HEAD_DIM) + EPS))
    return t * (_dot(rh, et) + _dot(rl, et)) * w


def _rope(t, cos, sin_a, sin_b):
    width = t.shape[1]
    rep = width // LANES
    tile = lambda u: jnp.concatenate([u] * rep, axis=1)
    return (t * tile(cos) + pltpu.roll(t, width - AXIS_DIM // 2, 1) * tile(sin_a)
            + pltpu.roll(t, AXIS_DIM // 2, 1) * tile(sin_b))


def _l0_inproj_kernel(*refs, latent):
    x_ref, mod_ref, nw_ref, w_ref, qn_ref, kn_ref, e_ref, et_ref = refs[:8]
    refs = refs[8:]
    if latent:
        cos_ref, sa_ref, sb_ref = refs[:3]
        q_ref, kk_ref, vt_ref, gs_ref = refs[3:]
    else:
        q_ref, kk_ref, vt_ref, gs_ref, knew_ref, vnew_ref = refs
    mod = mod_ref[0]
    h = _modulated_norm(x_ref[0], nw_ref[...], mod[:, :D_MODEL], mod[:, D_MODEL:2 * D_MODEL])
    proj = _dot(h.astype(bf16), w_ref[...])
    q = proj[:, :ATTN_WIDTH]
    k = proj[:, ATTN_WIDTH:ATTN_WIDTH + KV_WIDTH]
    v = proj[:, ATTN_WIDTH + KV_WIDTH:ATTN_WIDTH + 2 * KV_WIDTH]
    g = proj[:, ATTN_WIDTH + 2 * KV_WIDTH:]
    e = e_ref[...]
    et = et_ref[...]
    q = _head_rms(q, e, et, qn_ref[...])
    k = _head_rms(k, e[:KV_WIDTH], et[:, :KV_WIDTH], kn_ref[...])
    if latent:
        cos, sa, sb = cos_ref[...], sa_ref[...], sb_ref[...]
        q = _rope(q, cos, sa, sb)
        k = _rope(k, cos, sa, sb)
    else:
        knew_ref[0] = k
        vnew_ref[0] = v
    q_ref[0] = q.astype(bf16)
    gs_ref[0] = _silu(g).astype(bf16)
    rows = k.shape[0]
    low = lax.broadcasted_iota(jnp.int32, (rows, LANES), 1) < HEAD_DIM
    v_t = v.T
    ones = jnp.ones((VT_ROWS - HEAD_DIM, rows), bf16)
    for j in range(KV_WIDTH // LANES):
        kb = k[:, LANES * j:LANES * (j + 1)]
        kbs = pltpu.roll(kb, HEAD_DIM, 1)
        kk_ref[0, 2 * j] = jnp.where(low, kb, kbs).astype(bf16)
        kk_ref[0, 2 * j + 1] = jnp.where(low, kbs, kb).astype(bf16)
    for kvh in range(N_KV_HEADS):
        vt_ref[0, kvh, :HEAD_DIM, :] = v_t[HEAD_DIM * kvh:HEAD_DIM * (kvh + 1)].astype(bf16)
        vt_ref[0, kvh, HEAD_DIM:, :] = ones


def _l0_inproj(x, mod, norm_w, w_in, qn, kn, e, et, rope_tabs, *, latent, tm):
    b, t, _ = x.shape
    attn_in = w_in.shape[1]
    row = (lambda bi, i: (bi + 1, 0, 0)) if latent else (lambda bi, i: (0, 0, 0))
    const = lambda bi, i: (0, 0)
    in_specs = [pl.BlockSpec((1, tm, D_MODEL), lambda bi, i: (bi, i, 0)),
                pl.BlockSpec((1, 1, 3 * D_MODEL), row),
                pl.BlockSpec((1, D_MODEL), const),
                pl.BlockSpec((D_MODEL, attn_in), const),
                pl.BlockSpec((1, ATTN_WIDTH), const),
                pl.BlockSpec((1, KV_WIDTH), const),
                pl.BlockSpec((ATTN_WIDTH, LANES), const),
                pl.BlockSpec((LANES, ATTN_WIDTH), const)]
    args = [x, mod, norm_w, w_in, qn, kn, e, et]
    tok = lambda bi, i: (bi, i, 0)
    out_specs = [pl.BlockSpec((1, tm, ATTN_WIDTH), tok),
                 pl.BlockSpec((1, N_KV_HEADS, tm, LANES), lambda bi, i: (bi, 0, i, 0)),
                 pl.BlockSpec((1, N_KV_HEADS, VT_ROWS, tm), lambda bi, i: (bi, 0, 0, i)),
                 pl.BlockSpec((1, tm, ATTN_WIDTH), tok)]
    out_shape = [jax.ShapeDtypeStruct((b, t, ATTN_WIDTH), bf16),
                 jax.ShapeDtypeStruct((b, N_KV_HEADS, t, LANES), bf16),
                 jax.ShapeDtypeStruct((b, N_KV_HEADS, VT_ROWS, t), bf16),
                 jax.ShapeDtypeStruct((b, t, ATTN_WIDTH), bf16)]
    if latent:
        in_specs += [pl.BlockSpec((tm, LANES), lambda bi, i: (i, 0))] * 3
        args += list(rope_tabs)
    else:
        out_specs += [pl.BlockSpec((1, tm, KV_WIDTH), tok)] * 2
        out_shape += [jax.ShapeDtypeStruct((b, t, KV_WIDTH), f32)] * 2
    return pl.pallas_call(
        functools.partial(_l0_inproj_kernel, latent=latent),
        grid=(b, t // tm),
        in_specs=in_specs, out_specs=out_specs, out_shape=out_shape,
        compiler_params=_params(2),
        name="l0_inproj_latent" if latent else "l0_inproj_context",
    )(*args)


def _attn_kernel(*refs, n_src):
    q_ref = refs[0]
    kv_refs = refs[1:1 + 2 * n_src]
    o_ref = refs[-1]
    tq = q_ref.shape[1]
    low = lax.broadcasted_iota(jnp.int32, (tq, LANES), 1) < HEAD_DIM
    zero = jnp.zeros((tq, LANES), bf16)
    chunks = []
    for i in range(n_src):
        n_keys = kv_refs[2 * i].shape[2]
        width = min(KEY_CHUNK, n_keys)
        chunks += [(kv_refs[2 * i], kv_refs[2 * i + 1], slice(c0, c0 + width))
                   for c0 in range(0, n_keys, width)]
    steps = [(head, ch) for head in range(4) for ch in range(len(chunks))]

    def q_masked(head):
        qp = q_ref[0, :, LANES * (head // 2):LANES * (head // 2 + 1)]
        return jnp.where(low, qp, zero) if head % 2 == 0 else jnp.where(low, zero, qp)

    def scores(step):
        head, ch = step
        k_ref, _, keys = chunks[ch]
        return _dot_nt(k_ref[0, 0, keys, :], q_masked(head))

    halves = []
    m = acc = None
    queue = [scores(st) for st in steps[:SCORE_LOOKAHEAD]]
    for idx, (head, ch) in enumerate(steps):
        s = queue.pop(0)
        if idx + SCORE_LOOKAHEAD < len(steps):
            queue.append(scores(steps[idx + SCORE_LOOKAHEAD]))
        _, vt_ref, keys = chunks[ch]
        slab = s.shape[0] // 4
        m_c = jnp.maximum(jnp.maximum(s[:slab], s[slab:2 * slab]),
                          jnp.maximum(s[2 * slab:3 * slab], s[3 * slab:]))
        m_c = jnp.max(m_c, axis=0, keepdims=True)
        m_new = m_c if ch == 0 else jnp.maximum(m, m_c)
        pv = _dot(vt_ref[0, 0, :, keys], jnp.exp2(s - m_new).astype(bf16))
        acc = pv if ch == 0 else acc * jnp.exp2(m - m_new) + pv
        m = m_new
        if ch == len(chunks) - 1:
            halves.append(acc[:HEAD_DIM] / acc[HEAD_DIM:HEAD_DIM + 1])
            if head % 2 == 1:
                o_ref[0, :, LANES * (head // 2):LANES * (head // 2 + 1)] = (
                    jnp.concatenate(halves, axis=0).T.astype(bf16))
                halves = []


def _attention(q, kv_sources, *, tq):
    b, t, _ = q.shape
    in_specs = [pl.BlockSpec((1, tq, 2 * LANES), lambda bi, g, i: (bi, i, g))]
    args = [q]
    for arr in kv_sources:
        in_specs.append(pl.BlockSpec((1, 1) + arr.shape[2:], lambda bi, g, i: (bi, g, 0, 0)))
        args.append(arr)
    return pl.pallas_call(
        functools.partial(_attn_kernel, n_src=len(kv_sources) // 2),
        grid=(b, N_KV_HEADS, t // tq),
        in_specs=in_specs,
        out_specs=pl.BlockSpec((1, tq, 2 * LANES), lambda bi, g, i: (bi, i, g)),
        out_shape=jax.ShapeDtypeStruct((b, t, ATTN_WIDTH), bf16),
        compiler_params=_params(3),
        name="attention_latent" if len(kv_sources) > 2 else "attention_context",
    )(*args)


def _l0_outproj_kernel(x_ref, o_ref, gs_ref, mod_ref, w_ref, y_ref):
    gate = mod_ref[0][:, 2 * D_MODEL:]
    y_ref[0] = x_ref[0] + gate * _dot(o_ref[0] * gs_ref[0], w_ref[...])


def _l0_outproj(x, o, gs, mod, w_out, *, latent, tm):
    b, t, _ = x.shape
    row = (lambda bi, i: (bi + 1, 0, 0)) if latent else (lambda bi, i: (0, 0, 0))
    tok = lambda bi, i: (bi, i, 0)
    return pl.pallas_call(
        _l0_outproj_kernel,
        grid=(b, t // tm),
        in_specs=[pl.BlockSpec((1, tm, D_MODEL), tok),
                  pl.BlockSpec((1, tm, ATTN_WIDTH), tok),
                  pl.BlockSpec((1, tm, ATTN_WIDTH), tok),
                  pl.BlockSpec((1, 1, 3 * D_MODEL), row),
                  pl.BlockSpec((ATTN_WIDTH, D_MODEL), lambda bi, i: (0, 0))],
        out_specs=pl.BlockSpec((1, tm, D_MODEL), tok),
        out_shape=jax.ShapeDtypeStruct((b, t, D_MODEL), f32),
        compiler_params=_params(2),
        name="l0_outproj",
    )(x, o, gs, mod, w_out)


def _l1_inproj_kernel(x_ref, xp_ref, xn_ref, mod_ref, nw_ref, wzx_ref, wdt_ref, cw_ref, cb_ref,
                      dtb_ref, z_ref, xs_ref, bc_ref, dt_ref):
    i = pl.program_id(1)
    last = pl.num_programs(1) - 1
    tm = x_ref.shape[1]
    mod = mod_ref[0]
    xe = jnp.concatenate([xp_ref[0], x_ref[0], xn_ref[0]], axis=0)
    h = _modulated_norm(xe, nw_ref[...], mod[:, :D_MODEL], mod[:, D_MODEL:2 * D_MODEL]).astype(bf16)
    proj = _dot(h, wzx_ref[...])
    z_ref[0] = proj[HALO:HALO + tm, :D_INNER].astype(bf16)
    xbc = proj[:, D_INNER:]
    rows = xbc.shape[0]
    r = lax.broadcasted_iota(jnp.int32, (tm, 1), 0)
    cur = xbc[HALO:HALO + tm]
    prev = pltpu.roll(xbc, 1, 0)[HALO:HALO + tm]
    nxt = pltpu.roll(xbc, rows - 1, 0)[HALO:HALO + tm]
    prev = jnp.where(r == jnp.where(i == 0, 0, -1), 0.0, prev)
    nxt = jnp.where(r == jnp.where(i == last, tm - 1, -1), 0.0, nxt)
    cw = cw_ref[...]
    act = _silu(cw[0:1] * prev + cw[1:2] * cur + cw[2:3] * nxt + cb_ref[...])
    xs_ref[0] = act[:, :D_INNER].astype(bf16)
    bc_ref[0] = act[:, D_INNER:].astype(bf16)
    dt_ref[0] = _softplus(_dot(h[HALO:HALO + tm], wdt_ref[...]) + dtb_ref[...])


def _l1_inproj(x1, mod, norm_w, w_zx, w_dt, conv_w, conv_b, dt_bias, *, latent, tm):
    b, t, _ = x1.shape
    row = (lambda bi, i: (bi + 1, 0, 0)) if latent else (lambda bi, i: (0, 0, 0))
    const = lambda bi, i: (0, 0)
    tok = lambda bi, i: (bi, i, 0)
    hb = tm // HALO
    n_hb = t // HALO
    prev_map = lambda bi, i: (bi, jnp.maximum(i * hb - 1, 0), 0)
    next_map = lambda bi, i: (bi, jnp.minimum((i + 1) * hb, n_hb - 1), 0)
    return pl.pallas_call(
        _l1_inproj_kernel,
        grid=(b, t // tm),
        in_specs=[pl.BlockSpec((1, tm, D_MODEL), tok),
                  pl.BlockSpec((1, HALO, D_MODEL), prev_map),
                  pl.BlockSpec((1, HALO, D_MODEL), next_map),
                  pl.BlockSpec((1, 1, 3 * D_MODEL), row),
                  pl.BlockSpec((1, D_MODEL), const),
                  pl.BlockSpec((D_MODEL, D_INNER + CONV_DIM), const),
                  pl.BlockSpec((D_MODEL, LANES), const),
                  pl.BlockSpec((3, CONV_DIM), const),
                  pl.BlockSpec((1, CONV_DIM), const),
                  pl.BlockSpec((1, LANES), const)],
        out_specs=[pl.BlockSpec((1, tm, D_INNER), tok),
                   pl.BlockSpec((1, tm, D_INNER), tok),
                   pl.BlockSpec((1, tm, BC_WIDTH), tok),
                   pl.BlockSpec((1, tm, LANES), tok)],
        out_shape=[jax.ShapeDtypeStruct((b, t, D_INNER), bf16),
                   jax.ShapeDtypeStruct((b, t, D_INNER), bf16),
                   jax.ShapeDtypeStruct((b, t, BC_WIDTH), bf16),
                   jax.ShapeDtypeStruct((b, t, LANES), f32)],
        compiler_params=_params(2),
        name="l1_inproj",
    )(x1, x1, x1, mod, norm_w, w_zx, w_dt, conv_w, conv_b, dt_bias)


def _ssd_kernel(*refs, backward, has_h0, final_state):
    xs_ref, bc_ref, dt_ref, alog_ref, lmat_ref, umat_ref = refs[:6]
    refs = refs[6:]
    if has_h0:
        h0_ref, refs = refs[0], refs[1:]
    if backward:
        yf_ref, z_ref, x1_ref, mod_ref, gw_ref, wout_ref, dskip_ref = refs[:7]
        refs = refs[7:]
    out_ref, refs = refs[0], refs[1:]
    if final_state:
        hfin_ref, refs = refs[0], refs[1:]
    ht_ref = refs[0]
    if backward:
        yb_ref = refs[1]

    c = pl.program_id(1)
    n_pairs = SSD_HEADS // 2

    @pl.when(c == 0)
    def _init():
        if has_h0:
            for j in range(n_pairs):
                ht_ref[:, LANES * j:LANES * (j + 1)] = h0_ref[0, LANES * j:LANES * (j + 1), :].T
        else:
            ht_ref[...] = jnp.zeros_like(ht_ref)

    xs = xs_ref[0]
    dt = dt_ref[0]
    adt = dt * (-jnp.exp(alog_ref[...]))
    adt_t = adt.T
    dt_t = dt.T
    lower = lmat_ref[...]
    upper = umat_ref[...]
    ah, al = _split(adt)
    th, tl = _split(adt_t)
    if backward:
        acs = _dot(upper, ah) + _dot(upper, al)
        acs_t = _dot(th, lower) + _dot(tl, lower)
        end = 0
    else:
        acs = _dot(lower, ah) + _dot(lower, al)
        acs_t = _dot(th, upper) + _dot(tl, upper)
        end = CHUNK - 1
    eacs = jnp.exp(acs)
    chunk_decay = eacs[end:end + 1, :]
    w_t = dt_t * jnp.exp(acs_t[:, end:end + 1] - acs_t)
    ri = lax.broadcasted_iota(jnp.int32, (CHUNK, CHUNK), 0)
    ci = lax.broadcasted_iota(jnp.int32, (CHUNK, CHUNK), 1)
    mask = (ri <= ci) if backward else (ri >= ci)
    low = lax.broadcasted_iota(jnp.int32, (CHUNK, LANES), 1) < SSD_HEAD_DIM
    low_row = low[0:1]
    h_off = SSD_HEADS if backward else 0
    y_dst = yb_ref if backward else out_ref.at[0]

    for g in range(SSD_GROUPS):
        bm = bc_ref[0, :, D_STATE * g:D_STATE * (g + 1)]
        cm = bc_ref[0, :, D_STATE * (SSD_GROUPS + g):D_STATE * (SSD_GROUPS + g + 1)]
        cb = _dot_nt(cm, bm)
        bm_t = bm.astype(f32).T
        cm_f = cm.astype(f32)
        for jp in range(HEADS_PER_GROUP // 2):
            j = g * (HEADS_PER_GROUP // 2) + jp
            cols = slice(LANES * j, LANES * (j + 1))
            xp = xs[:, cols]
            rhs = jnp.concatenate([xp, ht_ref[:, cols].astype(bf16)], axis=0)
            ys, ss, cds = [], [], []
            for par in range(2):
                hh = h_off + 2 * j + par
                decay = jnp.exp(jnp.where(mask, acs[:, hh:hh + 1] - acs_t[hh:hh + 1, :], NEG_BIG))
                m_diag = (cb * decay * dt_t[hh:hh + 1, :]).astype(bf16)
                c_off = (cm_f * eacs[:, hh:hh + 1]).astype(bf16)
                ys.append(_dot(jnp.concatenate([m_diag, c_off], axis=1), rhs))
                ss.append(_dot((bm_t * w_t[hh:hh + 1, :]).astype(bf16), xp))
                cds.append(chunk_decay[:, hh:hh + 1])
            y_dst[:, cols] = jnp.where(low, ys[0], ys[1])
            ht_ref[:, cols] = (ht_ref[:, cols] * jnp.where(low_row, cds[0], cds[1])
                               + jnp.where(low, ss[0], ss[1]))

    if backward:
        y = yf_ref[0] + yb_ref[...] + dskip_ref[...] * xs.astype(f32)
        yz = y * _silu(z_ref[0].astype(f32))
        ms = jnp.mean(yz * yz, axis=-1, keepdims=True)
        yn = (yz * lax.rsqrt(ms + EPS) * gw_ref[...]).astype(bf16)
        gate = mod_ref[0][:, 2 * D_MODEL:]
        out_ref[0] = x1_ref[0] + gate * _dot(yn, wout_ref[...])

    if final_state:
        @pl.when(c == pl.num_programs(1) - 1)
        def _fin():
            for j in range(n_pairs):
                hfin_ref[0, LANES * j:LANES * (j + 1), :] = ht_ref[:, LANES * j:LANES * (j + 1)].T


def _ssd_pass(xs, bc, dt, alog, lmat, umat, h0, extra, *, backward, latent):
    b, t, _ = xs.shape
    nc = t // CHUNK
    has_h0 = h0 is not None
    final_state = not latent
    chunk = (lambda bi, c: (bi, nc - 1 - c, 0)) if backward else (lambda bi, c: (bi, c, 0))
    const = lambda bi, c: (0, 0)
    per_b = lambda bi, c: (bi, 0, 0)
    row = (lambda bi, c: (bi + 1, 0, 0)) if latent else (lambda bi, c: (0, 0, 0))
    in_specs = [pl.BlockSpec((1, CHUNK, D_INNER), chunk),
                pl.BlockSpec((1, CHUNK, BC_WIDTH), chunk),
                pl.BlockSpec((1, CHUNK, LANES), chunk),
                pl.BlockSpec((1, LANES), const),
                pl.BlockSpec((CHUNK, CHUNK), const),
                pl.BlockSpec((CHUNK, CHUNK), const)]
    args = [xs, bc, dt, alog, lmat, umat]
    if has_h0:
        in_specs.append(pl.BlockSpec((1, D_INNER, D_STATE), per_b))
        args.append(h0)
    scratch = [pltpu.VMEM((D_STATE, D_INNER), f32)]
    if backward:
        yf, z, x1, mod, gw, wout, dskip = extra
        in_specs += [pl.BlockSpec((1, CHUNK, D_INNER), chunk),
                     pl.BlockSpec((1, CHUNK, D_INNER), chunk),
                     pl.BlockSpec((1, CHUNK, D_MODEL), chunk),
                     pl.BlockSpec((1, 1, 3 * D_MODEL), row),
                     pl.BlockSpec((1, D_INNER), const),
                     pl.BlockSpec((D_INNER, D_MODEL), const),
                     pl.BlockSpec((1, D_INNER), const)]
        args += [yf, z, x1, mod, gw, wout, dskip]
        out_specs = [pl.BlockSpec((1, CHUNK, D_MODEL), chunk)]
        out_shape = [jax.ShapeDtypeStruct((b, t, D_MODEL), f32)]
        scratch.append(pltpu.VMEM((CHUNK, D_INNER), f32))
    else:
        out_specs = [pl.BlockSpec((1, CHUNK, D_INNER), chunk)]
        out_shape = [jax.ShapeDtypeStruct((b, t, D_INNER), f32)]
    if final_state:
        out_specs.append(pl.BlockSpec((1, D_INNER, D_STATE), per_b))
        out_shape.append(jax.ShapeDtypeStruct((b, D_INNER, D_STATE), f32))
    return pl.pallas_call(
        functools.partial(_ssd_kernel, backward=backward, has_h0=has_h0, final_state=final_state),
        grid=(b, nc),
        in_specs=in_specs, out_specs=out_specs, out_shape=out_shape,
        scratch_shapes=scratch,
        compiler_params=_params(2),
        name="ssd_bwd" if backward else "ssd_fwd",
    )(*args)


def _rope_tables(n_tokens):
    rows = n_tokens // GRID_W
    row_ids = jnp.repeat(jnp.arange(rows), GRID_W).astype(f32)
    col_ids = jnp.tile(jnp.arange(GRID_W), rows).astype(f32)
    inv_freq = 1.0 / (ROPE_THETA ** (jnp.arange(0, AXIS_DIM, 2, dtype=f32) / AXIS_DIM))
    ang = jnp.stack([row_ids[:, None] * inv_freq, col_ids[:, None] * inv_freq], axis=1)
    cos, sin, zero = jnp.cos(ang), jnp.sin(ang), jnp.zeros_like(ang)
    head = lambda a, b_: jnp.stack([a, b_], axis=2).reshape(n_tokens, HEAD_DIM)
    two = lambda u: jnp.concatenate([u, u], axis=1)
    return two(head(cos, cos)), two(head(-sin, zero)), two(head(zero, sin))


def kernel(x_prompt, x_sample, cache_k_l0, cache_v_l0, state_fwd_l1, state_bwd_l1, c, c_ctx,
           l0_norm_w, l0_mod_w, l0_mod_b, l0_w_in, l0_q_norm, l0_k_norm, l0_w_out,
           l1_norm_w, l1_mod_w, l1_mod_b, l1_w_in, l1_conv_w, l1_conv_b, l1_dt_bias_f, l1_dt_bias_b,
           l1_a_log_f, l1_a_log_b, l1_d_skip, l1_gnorm_w, l1_w_out):
    dec_b = x_sample.shape[0]
    pad = lambda u, n: jnp.pad(u, (0, n - u.shape[0]))

    n_rows = 16
    cond = jnp.zeros((n_rows, D_MODEL), f32).at[0].set(c_ctx).at[1:1 + dec_b].set(c)
    mod0 = _ada_mod(cond, l0_mod_w, l0_mod_b).reshape(n_rows, 1, 3 * D_MODEL)
    mod1 = _ada_mod(cond, l1_mod_w, l1_mod_b).reshape(n_rows, 1, 3 * D_MODEL)

    q_scale = HEAD_DIM ** -0.5 * math.log2(math.e)
    qn = (jnp.tile(l0_q_norm, N_HEADS) * q_scale).reshape(1, ATTN_WIDTH)
    kn = jnp.tile(l0_k_norm, N_KV_HEADS).reshape(1, KV_WIDTH)
    head_of = jnp.arange(ATTN_WIDTH) // HEAD_DIM
    e = (head_of[:, None] == jnp.arange(LANES)[None, :]).astype(bf16)
    et = e.T
    w_in0 = l0_w_in.astype(bf16)
    w_out0 = l0_w_out.astype(bf16)
    w_zx = l1_w_in[:, :D_INNER + CONV_DIM].astype(bf16)
    w_dt = jnp.pad(l1_w_in[:, D_INNER + CONV_DIM:], ((0, 0), (0, LANES - 2 * SSD_HEADS))).astype(bf16)
    w_out1 = l1_w_out.astype(bf16)
    dt_bias = pad(jnp.concatenate([l1_dt_bias_f, l1_dt_bias_b]), LANES).reshape(1, LANES)
    alog = pad(jnp.concatenate([l1_a_log_f, l1_a_log_b]), LANES).reshape(1, LANES)
    dskip = jnp.repeat(l1_d_skip, SSD_HEAD_DIM).reshape(1, D_INNER)
    nw0 = l0_norm_w.reshape(1, D_MODEL)
    nw1 = l1_norm_w.reshape(1, D_MODEL)
    gw = l1_gnorm_w.reshape(1, D_INNER)
    conv_b = l1_conv_b.reshape(1, CONV_DIM)
    tri = jnp.arange(CHUNK)
    lmat = (tri[None, :] <= tri[:, None]).astype(bf16)
    umat = lmat.T
    rope_tabs = _rope_tables(x_sample.shape[1])

    kc = cache_k_l0.transpose(0, 2, 1, 3).astype(bf16)
    vc = cache_v_l0.transpose(0, 2, 3, 1).astype(bf16)
    kk_ctx = jnp.concatenate([kc, kc], axis=-1)
    vt_ctx = jnp.concatenate(
        [vc, jnp.ones(vc.shape[:2] + (VT_ROWS - HEAD_DIM, vc.shape[3]), bf16)], axis=2)
    h0_f = state_fwd_l1.reshape(dec_b, D_INNER, D_STATE)
    h0_b = state_bwd_l1.reshape(dec_b, D_INNER, D_STATE)

    outs = {}
    for latent, x in ((False, x_prompt), (True, x_sample)):
        tm = 256
        res = _l0_inproj(x, mod0, nw0, w_in0, qn, kn, e, et, rope_tabs, latent=latent, tm=tm)
        if latent:
            q, kk, vt, gs = res
            kv = [kk_ctx, vt_ctx, kk, vt]
        else:
            q, kk, vt, gs, k_new, v_new = res
            kv = [kk, vt]
        o = _attention(q, kv, tq=256)
        x1 = _l0_outproj(x, o, gs, mod0, w_out0, latent=latent, tm=tm)
        z, xs, bc, dt = _l1_inproj(x1, mod1, nw1, w_zx, w_dt, l1_conv_w, conv_b, dt_bias,
                                   latent=latent, tm=tm)
        fwd = _ssd_pass(xs, bc, dt, alog, lmat, umat, h0_f if latent else None, None,
                        backward=False, latent=latent)
        bwd = _ssd_pass(xs, bc, dt, alog, lmat, umat, h0_b if latent else None,
                        (fwd[0], z, x1, mod1, gw, w_out1, dskip), backward=True, latent=latent)
        if latent:
            outs["y_sample"] = bwd[0]
        else:
            b = x.shape[0]
            outs["y_prompt"] = bwd[0]
            outs["k"] = k_new.reshape(b, -1, N_KV_HEADS, HEAD_DIM)
            outs["v"] = v_new.reshape(b, -1, N_KV_HEADS, HEAD_DIM)
            outs["hf"] = fwd[1].reshape(b, SSD_HEADS, SSD_HEAD_DIM, D_STATE)
            outs["hb"] = bwd[1].reshape(b, SSD_HEADS, SSD_HEAD_DIM, D_STATE)
    return (outs["y_prompt"], outs["y_sample"], outs["k"], outs["v"], outs["hf"], outs["hb"])
```

```python
import functools
import math

import jax
import jax.numpy as jnp
from jax import lax
from jax.experimental import pallas as pl
from jax.experimental.pallas import tpu as pltpu

f32 = jnp.float32
bf16 = jnp.bfloat16

D_MODEL = 1024
GRID_W = 64
EPS = 1e-6
N_HEADS = 16
N_KV_HEADS = 4
HEAD_DIM = 64
ATTN_WIDTH = N_HEADS * HEAD_DIM
KV_WIDTH = N_KV_HEADS * HEAD_DIM
AXIS_DIM = HEAD_DIM // 2
ROPE_THETA = 10000.0
D_INNER = 2048
SSD_HEAD_DIM = 64
SSD_HEADS = D_INNER // SSD_HEAD_DIM
SSD_GROUPS = 4
HEADS_PER_GROUP = SSD_HEADS // SSD_GROUPS
D_STATE = 128
CHUNK = 128
BC_WIDTH = 2 * SSD_GROUPS * D_STATE
CONV_DIM = D_INNER + BC_WIDTH

LANES = 128
BF16_ROWS = 16
HALO = BF16_ROWS
PROJ_BLOCK = 256
L0_SUB_ROWS = 256
SSD_STEP_CHUNKS = 2
HALF_VREG_ROWS = 8
VT_ROWS = HEAD_DIM + BF16_ROWS
KEY_CHUNK = 256
SCORE_LOOKAHEAD = 8
VMEM_LIMIT = 56 * 1024 * 1024

NEG_BIG = -1e30


def _params(n_axes):
    return pltpu.CompilerParams(
        dimension_semantics=("arbitrary",) * n_axes, vmem_limit_bytes=VMEM_LIMIT)


def _dot(a, b):
    return jnp.dot(a, b, preferred_element_type=f32)


def _dot_nt(a, b):
    return lax.dot_general(a, b, (((1,), (1,)), ((), ())), preferred_element_type=f32)


def _split(x):
    hi = x.astype(bf16)
    lo = (x - hi.astype(f32)).astype(bf16)
    return hi, lo


def _silu(x):
    return x * jax.nn.sigmoid(x)


def _softplus(x):
    return jnp.maximum(x, 0.0) + jnp.log1p(jnp.exp(-jnp.abs(x)))


def _modulated_norm(x, norm_w, shift, scale):
    ms = jnp.mean(x * x, axis=-1, keepdims=True)
    return (x * lax.rsqrt(ms + EPS) * norm_w) * (1.0 + scale) + shift


def _ada_mod_kernel(cond_ref, w_ref, b_ref, o_ref):
    s = _silu(cond_ref[...])
    sh, sl = _split(s)
    wh, wl = _split(w_ref[...])
    o_ref[...] = _dot(sh, wh) + (_dot(sh, wl) + _dot(sl, wh)) + b_ref[...]


def _ada_mod(cond, mod_w, mod_b):
    rows = cond.shape[0]
    tn = 1024
    return pl.pallas_call(
        _ada_mod_kernel,
        grid=(3 * D_MODEL // tn,),
        in_specs=[pl.BlockSpec((rows, D_MODEL), lambda j: (0, 0)),
                  pl.BlockSpec((D_MODEL, tn), lambda j: (0, j)),
                  pl.BlockSpec((1, tn), lambda j: (0, j))],
        out_specs=pl.BlockSpec((rows, tn), lambda j: (0, j)),
        out_shape=jax.ShapeDtypeStruct((rows, 3 * D_MODEL), f32),
        compiler_params=_params(1),
        name="ada_mod",
    )(cond, mod_w, mod_b.reshape(1, -1))


def _head_rms(t, e, et, w):
    hi, lo = _split(t * t)
    ss = _dot(hi, e) + _dot(lo, e)
    rh, rl = _split(lax.rsqrt(ss * (1.0 / HEAD_DIM) + EPS))
    return t * (_dot(rh, et) + _dot(rl, et)) * w


def _rope(t, cos, sin_a, sin_b):
    width = t.shape[1]
    rep = width // LANES
    tile = lambda u: jnp.concatenate([u] * rep, axis=1)
    return (t * tile(cos) + pltpu.roll(t, width - AXIS_DIM // 2, 1) * tile(sin_a)
            + pltpu.roll(t, AXIS_DIM // 2, 1) * tile(sin_b))


def _l0_inproj_kernel(*refs, latent):
    x_ref, mod_ref, nw_ref, w_ref, qn_ref, kn_ref, e_ref, et_ref = refs[:8]
    refs = refs[8:]
    if latent:
        cos_ref, sa_ref, sb_ref = refs[:3]
        q_ref, kk_ref, vt_ref, gs_ref = refs[3:]
    else:
        q_ref, kk_ref, vt_ref, gs_ref, knew_ref, vnew_ref = refs
    mod = mod_ref[0]
    e = e_ref[...]
    et = et_ref[...]
    subs = [slice(r0, r0 + L0_SUB_ROWS) for r0 in range(0, x_ref.shape[1], L0_SUB_ROWS)]
    projs = []
    for rows in subs:
        h = _modulated_norm(x_ref[0, rows, :], nw_ref[...], mod[:, :D_MODEL], mod[:, D_MODEL:2 * D_MODEL])
        projs.append(_dot(h.astype(bf16), w_ref[...]))
    low = lax.broadcasted_iota(jnp.int32, (L0_SUB_ROWS, LANES), 1) < HEAD_DIM
    ones = jnp.ones((VT_ROWS - HEAD_DIM, L0_SUB_ROWS), bf16)
    for rows, proj in zip(subs, projs):
        q = proj[:, :ATTN_WIDTH]
        k = proj[:, ATTN_WIDTH:ATTN_WIDTH + KV_WIDTH]
        v = proj[:, ATTN_WIDTH + KV_WIDTH:ATTN_WIDTH + 2 * KV_WIDTH]
        g = proj[:, ATTN_WIDTH + 2 * KV_WIDTH:]
        q = _head_rms(q, e, et, qn_ref[...])
        k = _head_rms(k, e[:KV_WIDTH], et[:, :KV_WIDTH], kn_ref[...])
        if latent:
            cos, sa, sb = cos_ref[rows, :], sa_ref[rows, :], sb_ref[rows, :]
            q = _rope(q, cos, sa, sb)
            k = _rope(k, cos, sa, sb)
        else:
            knew_ref[0, rows, :] = k
            vnew_ref[0, rows, :] = v
        q_ref[0, rows, :] = q.astype(bf16)
        gs_ref[0, rows, :] = _silu(g).astype(bf16)
        v_t = v.T
        for j in range(KV_WIDTH // LANES):
            kb = k[:, LANES * j:LANES * (j + 1)]
            kbs = pltpu.roll(kb, HEAD_DIM, 1)
            kk_ref[0, 2 * j, rows, :] = jnp.where(low, kb, kbs).astype(bf16)
            kk_ref[0, 2 * j + 1, rows, :] = jnp.where(low, kbs, kb).astype(bf16)
        for kvh in range(N_KV_HEADS):
            vt_ref[0, kvh, :HEAD_DIM, rows] = v_t[HEAD_DIM * kvh:HEAD_DIM * (kvh + 1)].astype(bf16)
            vt_ref[0, kvh, HEAD_DIM:, rows] = ones


def _l0_inproj(x, mod, norm_w, w_in, qn, kn, e, et, rope_tabs, *, latent, tm):
    b, t, _ = x.shape
    attn_in = w_in.shape[1]
    row = (lambda bi, i: (bi + 1, 0, 0)) if latent else (lambda bi, i: (0, 0, 0))
    const = lambda bi, i: (0, 0)
    in_specs = [pl.BlockSpec((1, tm, D_MODEL), lambda bi, i: (bi, i, 0)),
                pl.BlockSpec((1, 1, 3 * D_MODEL), row),
                pl.BlockSpec((1, D_MODEL), const),
                pl.BlockSpec((D_MODEL, attn_in), const),
                pl.BlockSpec((1, ATTN_WIDTH), const),
                pl.BlockSpec((1, KV_WIDTH), const),
                pl.BlockSpec((ATTN_WIDTH, LANES), const),
                pl.BlockSpec((LANES, ATTN_WIDTH), const)]
    args = [x, mod, norm_w, w_in, qn, kn, e, et]
    tok = lambda bi, i: (bi, i, 0)
    out_specs = [pl.BlockSpec((1, tm, ATTN_WIDTH), tok),
                 pl.BlockSpec((1, N_KV_HEADS, tm, LANES), lambda bi, i: (bi, 0, i, 0)),
                 pl.BlockSpec((1, N_KV_HEADS, VT_ROWS, tm), lambda bi, i: (bi, 0, 0, i)),
                 pl.BlockSpec((1, tm, ATTN_WIDTH), tok)]
    out_shape = [jax.ShapeDtypeStruct((b, t, ATTN_WIDTH), bf16),
                 jax.ShapeDtypeStruct((b, N_KV_HEADS, t, LANES), bf16),
                 jax.ShapeDtypeStruct((b, N_KV_HEADS, VT_ROWS, t), bf16),
                 jax.ShapeDtypeStruct((b, t, ATTN_WIDTH), bf16)]
    if latent:
        in_specs += [pl.BlockSpec((tm, LANES), lambda bi, i: (i, 0))] * 3
        args += list(rope_tabs)
    else:
        out_specs += [pl.BlockSpec((1, tm, KV_WIDTH), tok)] * 2
        out_shape += [jax.ShapeDtypeStruct((b, t, KV_WIDTH), f32)] * 2
    return pl.pallas_call(
        functools.partial(_l0_inproj_kernel, latent=latent),
        grid=(b, t // tm),
        in_specs=in_specs, out_specs=out_specs, out_shape=out_shape,
        compiler_params=_params(2),
        name="l0_inproj_latent" if latent else "l0_inproj_context",
    )(*args)


def _attn_kernel(*refs, n_src):
    q_ref = refs[0]
    kv_refs = refs[1:1 + 2 * n_src]
    o_ref = refs[-1]
    tq = q_ref.shape[1]
    low = lax.broadcasted_iota(jnp.int32, (tq, LANES), 1) < HEAD_DIM
    zero = jnp.zeros((tq, LANES), bf16)
    chunks = []
    for i in range(n_src):
        n_keys = kv_refs[2 * i].shape[2]
        width = min(KEY_CHUNK, n_keys)
        chunks += [(kv_refs[2 * i], kv_refs[2 * i + 1], slice(c0, c0 + width))
                   for c0 in range(0, n_keys, width)]
    steps = [(head, ch) for head in range(4) for ch in range(len(chunks))]

    def q_masked(head):
        qp = q_ref[0, :, LANES * (head // 2):LANES * (head // 2 + 1)]
        return jnp.where(low, qp, zero) if head % 2 == 0 else jnp.where(low, zero, qp)

    def scores(step):
        head, ch = step
        k_ref, _, keys = chunks[ch]
        return _dot_nt(k_ref[0, 0, keys, :], q_masked(head))

    halves = []
    m = acc = None
    queue = [scores(st) for st in steps[:SCORE_LOOKAHEAD]]
    for idx, (head, ch) in enumerate(steps):
        s = queue.pop(0)
        if idx + SCORE_LOOKAHEAD < len(steps):
            queue.append(scores(steps[idx + SCORE_LOOKAHEAD]))
        _, vt_ref, keys = chunks[ch]
        slab = s.shape[0] // 4
        m_c = jnp.maximum(jnp.maximum(s[:slab], s[slab:2 * slab]),
                          jnp.maximum(s[2 * slab:3 * slab], s[3 * slab:]))
        m_c = jnp.max(m_c, axis=0, keepdims=True)
        m_new = m_c if ch == 0 else jnp.maximum(m, m_c)
        pv = _dot(vt_ref[0, 0, :, keys], jnp.exp2(s - m_new).astype(bf16))
        acc = pv if ch == 0 else acc * jnp.exp2(m - m_new) + pv
        m = m_new
        if ch == len(chunks) - 1:
            halves.append(acc[:HEAD_DIM] / acc[HEAD_DIM:HEAD_DIM + 1])
            if head % 2 == 1:
                o_ref[0, :, LANES * (head // 2):LANES * (head // 2 + 1)] = (
                    jnp.concatenate(halves, axis=0).T.astype(bf16))
                halves = []


def _attention(q, kv_sources, *, tq):
    b, t, _ = q.shape
    in_specs = [pl.BlockSpec((1, tq, 2 * LANES), lambda bi, g, i: (bi, i, g))]
    args = [q]
    for arr in kv_sources:
        in_specs.append(pl.BlockSpec((1, 1) + arr.shape[2:], lambda bi, g, i: (bi, g, 0, 0)))
        args.append(arr)
    return pl.pallas_call(
        functools.partial(_attn_kernel, n_src=len(kv_sources) // 2),
        grid=(b, N_KV_HEADS, t // tq),
        in_specs=in_specs,
        out_specs=pl.BlockSpec((1, tq, 2 * LANES), lambda bi, g, i: (bi, i, g)),
        out_shape=jax.ShapeDtypeStruct((b, t, ATTN_WIDTH), bf16),
        compiler_params=_params(3),
        name="attention_latent" if len(kv_sources) > 2 else "attention_context",
    )(*args)


def _mid_kernel(x_ref, xp_ref, xn_ref, o_ref, op_ref, on_ref, g_ref, gp_ref, gn_ref,
                mod0_ref, mod1_ref, wout_ref, nw_ref, wzx_ref, wdt_ref, cw_ref, cb_ref, dtb_ref,
                alog_ref, lmat_ref, umat_ref,
                x1_ref, z_ref, xs_ref, bc_ref, acs_ref, rowt_ref, w_ref):
    i = pl.program_id(1)
    last = pl.num_programs(1) - 1
    tm = x_ref.shape[1]
    rows = tm + 2 * HALO
    body = slice(HALO, HALO + tm)
    cat = lambda a, b_, c: jnp.concatenate([a[0], b_[0], c[0]], axis=0)
    gated = cat(op_ref, o_ref, on_ref) * cat(gp_ref, g_ref, gn_ref)
    x1 = cat(xp_ref, x_ref, xn_ref) + mod0_ref[0][:, 2 * D_MODEL:] * _dot(gated, wout_ref[...])
    x1_ref[0] = x1[body]
    mod = mod1_ref[0]
    h = _modulated_norm(x1, nw_ref[...], mod[:, :D_MODEL], mod[:, D_MODEL:2 * D_MODEL])
    h_body = h[body].astype(bf16)
    h = jnp.concatenate([jnp.where(i == 0, 0.0, h[:HALO]).astype(bf16), h_body,
                         jnp.where(i == last, 0.0, h[HALO + tm:]).astype(bf16)], axis=0)

    n_z = D_INNER // PROJ_BLOCK
    for j in range(CONV_DIM // PROJ_BLOCK):
        cols = slice(PROJ_BLOCK * j, PROJ_BLOCK * (j + 1))
        blk = _dot(h, wzx_ref[:, D_INNER + PROJ_BLOCK * j:D_INNER + PROJ_BLOCK * (j + 1)])
        act = _silu(cw_ref[0:1, cols] * pltpu.roll(blk, 1, 0)[body] + cw_ref[1:2, cols] * blk[body]
                    + cw_ref[2:3, cols] * pltpu.roll(blk, rows - 1, 0)[body]
                    + cb_ref[:, cols]).astype(bf16)
        if j < n_z:
            xs_ref[0, :, cols] = act
            z_ref[0, :, cols] = _dot(h_body, wzx_ref[:, cols]).astype(bf16)
        else:
            bc_ref[0, :, PROJ_BLOCK * (j - n_z):PROJ_BLOCK * (j - n_z + 1)] = act

    dt = _softplus(_dot(h_body, wdt_ref[...]) + dtb_ref[...])
    a = -jnp.exp(alog_ref[...])
    fwd_lane = lax.broadcasted_iota(jnp.int32, (CHUNK, LANES), 1) < SSD_HEADS
    lower, upper = lmat_ref[...], umat_ref[...]
    for c in range(tm // CHUNK):
        dtc = dt[CHUNK * c:CHUNK * (c + 1)]
        ah, al = _split(dtc * a)
        acs = jnp.where(fwd_lane, _dot(lower, ah) + _dot(lower, al), _dot(upper, ah) + _dot(upper, al))
        end_row = jnp.where(fwd_lane[0:1], acs[CHUNK - 1:CHUNK], acs[0:1])
        acs_ref[0, CHUNK * c:CHUNK * (c + 1)] = acs
        w_ref[0, CHUNK * c:CHUNK * (c + 1)] = dtc * jnp.exp(end_row - acs)
        rowt_ref[0, c] = (acs - jnp.log(dtc)).T


def _mid(x, o, gs, mod0, mod1, w_out0, norm_w, w_zx, w_dt, conv_w, conv_b, dt_bias, alog, lmat, umat,
         *, latent, tm):
    b, t, _ = x.shape
    row = (lambda bi, i: (bi + 1, 0, 0)) if latent else (lambda bi, i: (0, 0, 0))
    const = lambda bi, i: (0, 0)
    tok = lambda bi, i: (bi, i, 0)
    hb = tm // HALO
    n_hb = t // HALO
    prev_map = lambda bi, i: (bi, jnp.maximum(i * hb - 1, 0), 0)
    next_map = lambda bi, i: (bi, jnp.minimum((i + 1) * hb, n_hb - 1), 0)
    halo3 = lambda width: [pl.BlockSpec((1, tm, width), tok),
                           pl.BlockSpec((1, HALO, width), prev_map),
                           pl.BlockSpec((1, HALO, width), next_map)]
    resident = lambda shape: pl.BlockSpec(shape, const, pipeline_mode=pl.Buffered(1))
    chunk4 = lambda bi, i: (bi, i, 0, 0)
    ncs = tm // CHUNK
    return pl.pallas_call(
        _mid_kernel,
        grid=(b, t // tm),
        in_specs=halo3(D_MODEL) + halo3(ATTN_WIDTH) + halo3(ATTN_WIDTH) + [
            pl.BlockSpec((1, 1, 3 * D_MODEL), row),
            pl.BlockSpec((1, 1, 3 * D_MODEL), row),
            resident((ATTN_WIDTH, D_MODEL)),
            pl.BlockSpec((1, D_MODEL), const),
            resident((D_MODEL, D_INNER + CONV_DIM)),
            resident((D_MODEL, LANES)),
            pl.BlockSpec((3, CONV_DIM), const),
            pl.BlockSpec((1, CONV_DIM), const),
            pl.BlockSpec((1, LANES), const),
            pl.BlockSpec((1, LANES), const),
            pl.BlockSpec((CHUNK, CHUNK), const),
            pl.BlockSpec((CHUNK, CHUNK), const)],
        out_specs=[pl.BlockSpec((1, tm, D_MODEL), tok),
                   pl.BlockSpec((1, tm, D_INNER), tok),
                   pl.BlockSpec((1, tm, D_INNER), tok),
                   pl.BlockSpec((1, tm, BC_WIDTH), tok),
                   pl.BlockSpec((1, tm, LANES), tok),
                   pl.BlockSpec((1, ncs, LANES, CHUNK), chunk4),
                   pl.BlockSpec((1, tm, LANES), tok)],
        out_shape=[jax.ShapeDtypeStruct((b, t, D_MODEL), f32),
                   jax.ShapeDtypeStruct((b, t, D_INNER), bf16),
                   jax.ShapeDtypeStruct((b, t, D_INNER), bf16),
                   jax.ShapeDtypeStruct((b, t, BC_WIDTH), bf16),
                   jax.ShapeDtypeStruct((b, t, LANES), f32),
                   jax.ShapeDtypeStruct((b, t // CHUNK, LANES, CHUNK), f32),
                   jax.ShapeDtypeStruct((b, t, LANES), f32)],
        compiler_params=_params(2),
        name="mid_latent" if latent else "mid_context",
    )(x, x, x, o, o, o, gs, gs, gs, mod0, mod1, w_out0, norm_w, w_zx, w_dt, conv_w, conv_b,
      dt_bias, alog, lmat, umat)


def _ssd_kernel(*refs, backward, has_h0, final_state):
    xs_ref, bc_ref, acs_ref, rowt_ref, w_ref, e_ref = refs[:6]
    refs = refs[6:]
    if has_h0:
        h0_ref, refs = refs[0], refs[1:]
    if backward:
        yf_ref, z_ref, x1_ref, mod_ref, gw_ref, wout_ref, dskip_ref = refs[:7]
        refs = refs[7:]
    out_ref, refs = refs[0], refs[1:]
    if final_state:
        hfin_ref, refs = refs[0], refs[1:]
    ht_ref = refs[0]
    if backward:
        yb_ref, yn_ref = refs[1:3]

    step = pl.program_id(1)
    n_pairs = SSD_HEADS // 2
    pairs_per_group = HEADS_PER_GROUP // 2
    group_w = HEADS_PER_GROUP * SSD_HEAD_DIM
    n_chunks = xs_ref.shape[1] // CHUNK

    @pl.when(step == 0)
    def _init():
        if has_h0:
            for j in range(n_pairs):
                ht_ref[:, LANES * j:LANES * (j + 1)] = h0_ref[0, LANES * j:LANES * (j + 1), :].T
        else:
            ht_ref[...] = jnp.zeros_like(ht_ref)

    h_off = SSD_HEADS if backward else 0
    end = 0 if backward else CHUNK - 1
    ri = lax.broadcasted_iota(jnp.int32, (CHUNK, CHUNK), 0)
    ci = lax.broadcasted_iota(jnp.int32, (CHUNK, CHUNK), 1)
    mask = (ri <= ci) if backward else (ri >= ci)
    low = lax.broadcasted_iota(jnp.int32, (CHUNK, LANES), 1) < SSD_HEAD_DIM

    for k in (range(n_chunks - 1, -1, -1) if backward else range(n_chunks)):
        rows = slice(CHUNK * k, CHUNK * (k + 1))
        acs = acs_ref[0, rows, :]
        rowt = rowt_ref[0, k][h_off:h_off + SSD_HEADS]
        w_b = w_ref[0, rows, :].astype(bf16)
        cd_h, cd_l = _split(jnp.broadcast_to(jnp.exp(acs[end:end + 1, :]), (HALF_VREG_ROWS, LANES)))
        chunk_decay = (_dot(cd_h, e_ref[...]) + _dot(cd_l, e_ref[...]))[0:1]
        y_dst = yb_ref if backward else out_ref.at[0, rows]

        for g in range(SSD_GROUPS):
            gcols = slice(group_w * g, group_w * (g + 1))
            bm = bc_ref[0, rows, D_STATE * g:D_STATE * (g + 1)]
            cm = bc_ref[0, rows, D_STATE * (SSD_GROUPS + g):D_STATE * (SSD_GROUPS + g + 1)]
            cb = _dot_nt(cm, bm)
            cm_f = cm.astype(f32)
            xw = xs_ref[0, rows, gcols] * _dot(w_b, e_ref[:, gcols]).astype(bf16)
            st = _dot(bm.astype(f32).T.astype(bf16), xw)
            for jp in range(pairs_per_group):
                j = g * pairs_per_group + jp
                cols = slice(LANES * j, LANES * (j + 1))
                rhs = jnp.concatenate([xs_ref[0, rows, cols], ht_ref[:, cols].astype(bf16)], axis=0)
                ys = []
                for par in range(2):
                    hl = 2 * j + par
                    col = jnp.broadcast_to(acs[:, h_off + hl:h_off + hl + 1], (CHUNK, CHUNK))
                    decay = jnp.exp(jnp.where(mask, col - rowt[hl:hl + 1, :], NEG_BIG))
                    m_diag = (cb * decay).astype(bf16)
                    c_off = (cm_f * jnp.exp(col)).astype(bf16)
                    ys.append(_dot(jnp.concatenate([m_diag, c_off], axis=1), rhs))
                y_dst[:, cols] = jnp.where(low, ys[0], ys[1])
                ht_ref[:, cols] = (ht_ref[:, cols] * chunk_decay[:, cols]
                                   + st[:, LANES * jp:LANES * (jp + 1)])

        if backward:
            y = yf_ref[0, rows, :] + yb_ref[...] + dskip_ref[...] * xs_ref[0, rows, :].astype(f32)
            yz = y * _silu(z_ref[0, rows, :].astype(f32))
            ms = jnp.mean(yz * yz, axis=-1, keepdims=True)
            yn_ref[rows, :] = (yz * lax.rsqrt(ms + EPS) * gw_ref[...]).astype(bf16)

    if backward:
        gate = mod_ref[0][:, 2 * D_MODEL:]
        out_ref[0] = x1_ref[0] + gate * _dot(yn_ref[...], wout_ref[...])

    if final_state:
        @pl.when(step == pl.num_programs(1) - 1)
        def _fin():
            for j in range(n_pairs):
                hfin_ref[0, LANES * j:LANES * (j + 1), :] = ht_ref[:, LANES * j:LANES * (j + 1)].T


def _ssd_pass(xs, bc, acs, rowt, w, e_dir, h0, extra, *, backward, latent):
    b, t, _ = xs.shape
    rows = SSD_STEP_CHUNKS * CHUNK
    n_steps = t // rows
    has_h0 = h0 is not None
    final_state = not latent
    blk = (lambda bi, c: (bi, n_steps - 1 - c, 0)) if backward else (lambda bi, c: (bi, c, 0))
    blk4 = (lambda bi, c: (bi, n_steps - 1 - c, 0, 0)) if backward else (lambda bi, c: (bi, c, 0, 0))
    const = lambda bi, c: (0, 0)
    per_b = lambda bi, c: (bi, 0, 0)
    row = (lambda bi, c: (bi + 1, 0, 0)) if latent else (lambda bi, c: (0, 0, 0))
    in_specs = [pl.BlockSpec((1, rows, D_INNER), blk),
                pl.BlockSpec((1, rows, BC_WIDTH), blk),
                pl.BlockSpec((1, rows, LANES), blk),
                pl.BlockSpec((1, SSD_STEP_CHUNKS, LANES, CHUNK), blk4),
                pl.BlockSpec((1, rows, LANES), blk),
                pl.BlockSpec((LANES, D_INNER), const)]
    args = [xs, bc, acs, rowt, w, e_dir]
    if has_h0:
        in_specs.append(pl.BlockSpec((1, D_INNER, D_STATE), per_b))
        args.append(h0)
    scratch = [pltpu.VMEM((D_STATE, D_INNER), f32)]
    if backward:
        yf, z, x1, mod, gw, wout, dskip = extra
        in_specs += [pl.BlockSpec((1, rows, D_INNER), blk),
                     pl.BlockSpec((1, rows, D_INNER), blk),
                     pl.BlockSpec((1, rows, D_MODEL), blk),
                     pl.BlockSpec((1, 1, 3 * D_MODEL), row),
                     pl.BlockSpec((1, D_INNER), const),
                     pl.BlockSpec((D_INNER, D_MODEL), const),
                     pl.BlockSpec((1, D_INNER), const)]
        args += [yf, z, x1, mod, gw, wout, dskip]
        out_specs = [pl.BlockSpec((1, rows, D_MODEL), blk)]
        out_shape = [jax.ShapeDtypeStruct((b, t, D_MODEL), f32)]
        scratch += [pltpu.VMEM((CHUNK, D_INNER), f32), pltpu.VMEM((rows, D_INNER), bf16)]
    else:
        out_specs = [pl.BlockSpec((1, rows, D_INNER), blk)]
        out_shape = [jax.ShapeDtypeStruct((b, t, D_INNER), f32)]
    if final_state:
        out_specs.append(pl.BlockSpec((1, D_INNER, D_STATE), per_b))
        out_shape.append(jax.ShapeDtypeStruct((b, D_INNER, D_STATE), f32))
    return pl.pallas_call(
        functools.partial(_ssd_kernel, backward=backward, has_h0=has_h0, final_state=final_state),
        grid=(b, n_steps),
        in_specs=in_specs, out_specs=out_specs, out_shape=out_shape,
        scratch_shapes=scratch,
        compiler_params=_params(2),
        name="ssd_bwd" if backward else "ssd_fwd",
    )(*args)


def _rope_tables(n_tokens):
    rows = n_tokens // GRID_W
    row_ids = jnp.repeat(jnp.arange(rows), GRID_W).astype(f32)
    col_ids = jnp.tile(jnp.arange(GRID_W), rows).astype(f32)
    inv_freq = 1.0 / (ROPE_THETA ** (jnp.arange(0, AXIS_DIM, 2, dtype=f32) / AXIS_DIM))
    ang = jnp.stack([row_ids[:, None] * inv_freq, col_ids[:, None] * inv_freq], axis=1)
    cos, sin, zero = jnp.cos(ang), jnp.sin(ang), jnp.zeros_like(ang)
    head = lambda a, b_: jnp.stack([a, b_], axis=2).reshape(n_tokens, HEAD_DIM)
    two = lambda u: jnp.concatenate([u, u], axis=1)
    return two(head(cos, cos)), two(head(-sin, zero)), two(head(zero, sin))


def kernel(x_prompt, x_sample, cache_k_l0, cache_v_l0, state_fwd_l1, state_bwd_l1, c, c_ctx,
           l0_norm_w, l0_mod_w, l0_mod_b, l0_w_in, l0_q_norm, l0_k_norm, l0_w_out,
           l1_norm_w, l1_mod_w, l1_mod_b, l1_w_in, l1_conv_w, l1_conv_b, l1_dt_bias_f, l1_dt_bias_b,
           l1_a_log_f, l1_a_log_b, l1_d_skip, l1_gnorm_w, l1_w_out):
    dec_b = x_sample.shape[0]
    pad = lambda u, n: jnp.pad(u, (0, n - u.shape[0]))

    n_rows = 16
    cond = jnp.zeros((n_rows, D_MODEL), f32).at[0].set(c_ctx).at[1:1 + dec_b].set(c)
    mod0 = _ada_mod(cond, l0_mod_w, l0_mod_b).reshape(n_rows, 1, 3 * D_MODEL)
    mod1 = _ada_mod(cond, l1_mod_w, l1_mod_b).reshape(n_rows, 1, 3 * D_MODEL)

    q_scale = HEAD_DIM ** -0.5 * math.log2(math.e)
    qn = (jnp.tile(l0_q_norm, N_HEADS) * q_scale).reshape(1, ATTN_WIDTH)
    kn = jnp.tile(l0_k_norm, N_KV_HEADS).reshape(1, KV_WIDTH)
    head_of = jnp.arange(ATTN_WIDTH) // HEAD_DIM
    e = (head_of[:, None] == jnp.arange(LANES)[None, :]).astype(bf16)
    et = e.T
    w_in0 = l0_w_in.astype(bf16)
    w_out0 = l0_w_out.astype(bf16)
    w_zx = l1_w_in[:, :D_INNER + CONV_DIM].astype(bf16)
    w_dt = jnp.pad(l1_w_in[:, D_INNER + CONV_DIM:], ((0, 0), (0, LANES - 2 * SSD_HEADS))).astype(bf16)
    w_out1 = l1_w_out.astype(bf16)
    dt_bias = pad(jnp.concatenate([l1_dt_bias_f, l1_dt_bias_b]), LANES).reshape(1, LANES)
    alog = pad(jnp.concatenate([l1_a_log_f, l1_a_log_b]), LANES).reshape(1, LANES)
    dskip = jnp.repeat(l1_d_skip, SSD_HEAD_DIM).reshape(1, D_INNER)
    nw0 = l0_norm_w.reshape(1, D_MODEL)
    nw1 = l1_norm_w.reshape(1, D_MODEL)
    gw = l1_gnorm_w.reshape(1, D_INNER)
    conv_b = l1_conv_b.reshape(1, CONV_DIM)
    tri = jnp.arange(CHUNK)
    lmat = (tri[None, :] <= tri[:, None]).astype(bf16)
    umat = lmat.T
    rope_tabs = _rope_tables(x_sample.shape[1])
    ssd_head_of = jnp.arange(D_INNER) // SSD_HEAD_DIM
    e_fwd = (jnp.arange(LANES)[:, None] == ssd_head_of[None, :]).astype(bf16)
    e_bwd = (jnp.arange(LANES)[:, None] == SSD_HEADS + ssd_head_of[None, :]).astype(bf16)

    kc = cache_k_l0.transpose(0, 2, 1, 3).astype(bf16)
    vc = cache_v_l0.transpose(0, 2, 3, 1).astype(bf16)
    kk_ctx = jnp.concatenate([kc, kc], axis=-1)
    vt_ctx = jnp.concatenate(
        [vc, jnp.ones(vc.shape[:2] + (VT_ROWS - HEAD_DIM, vc.shape[3]), bf16)], axis=2)
    h0_f = state_fwd_l1.reshape(dec_b, D_INNER, D_STATE)
    h0_b = state_bwd_l1.reshape(dec_b, D_INNER, D_STATE)

    outs = {}
    for latent, x in ((False, x_prompt), (True, x_sample)):
        res = _l0_inproj(x, mod0, nw0, w_in0, qn, kn, e, et, rope_tabs, latent=latent,
                         tm=2 * L0_SUB_ROWS if latent else L0_SUB_ROWS)
        if latent:
            q, kk, vt, gs = res
            kv = [kk_ctx, vt_ctx, kk, vt]
        else:
            q, kk, vt, gs, k_new, v_new = res
            kv = [kk, vt]
        o = _attention(q, kv, tq=256)
        x1, z, xs, bc, acs, rowt, w = _mid(
            x, o, gs, mod0, mod1, w_out0, nw1, w_zx, w_dt, l1_conv_w, conv_b, dt_bias, alog,
            lmat, umat, latent=latent, tm=512 if latent else 256)
        fwd = _ssd_pass(xs, bc, acs, rowt, w, e_fwd, h0_f if latent else None, None,
                        backward=False, latent=latent)
        bwd = _ssd_pass(xs, bc, acs, rowt, w, e_bwd, h0_b if latent else None,
                        (fwd[0], z, x1, mod1, gw, w_out1, dskip), backward=True, latent=latent)
        if latent:
            outs["y_sample"] = bwd[0]
        else:
            b = x.shape[0]
            outs["y_prompt"] = bwd[0]
            outs["k"] = k_new.reshape(b, -1, N_KV_HEADS, HEAD_DIM)
            outs["v"] = v_new.reshape(b, -1, N_KV_HEADS, HEAD_DIM)
            outs["hf"] = fwd[1].reshape(b, SSD_HEADS, SSD_HEAD_DIM, D_STATE)
            outs["hb"] = bwd[1].reshape(b, SSD_HEADS, SSD_HEAD_DIM, D_STATE)
    return (outs["y_prompt"], outs["y_sample"], outs["k"], outs["v"], outs["hf"], outs["hb"])
```

```python
import functools
import math

import jax
import jax.numpy as jnp
from jax import lax
from jax.experimental import pallas as pl
from jax.experimental.pallas import tpu as pltpu

f32 = jnp.float32
bf16 = jnp.bfloat16

D_MODEL = 1024
GRID_W = 64
EPS = 1e-6
N_HEADS = 16
N_KV_HEADS = 4
HEAD_DIM = 64
ATTN_WIDTH = N_HEADS * HEAD_DIM
KV_WIDTH = N_KV_HEADS * HEAD_DIM
AXIS_DIM = HEAD_DIM // 2
ROPE_THETA = 10000.0
D_INNER = 2048
SSD_HEAD_DIM = 64
SSD_HEADS = D_INNER // SSD_HEAD_DIM
SSD_GROUPS = 4
HEADS_PER_GROUP = SSD_HEADS // SSD_GROUPS
D_STATE = 128
CHUNK = 128
BC_WIDTH = 2 * SSD_GROUPS * D_STATE
CONV_DIM = D_INNER + BC_WIDTH

LANES = 128
BF16_ROWS = 16
HALO = BF16_ROWS
PROJ_BLOCK = 256
PROJ_LOOKAHEAD = 2
L0_SUB_ROWS = 256
SSD_STEP_CHUNKS = 2
HALF_VREG_ROWS = 8
VT_ROWS = HEAD_DIM + BF16_ROWS
Q_SUB_ROWS = 256
KEY_CHUNK = 256
SCORE_LOOKAHEAD = 8
VMEM_LIMIT = 56 * 1024 * 1024

NEG_BIG = -1e30


def _params(n_axes):
    return pltpu.CompilerParams(
        dimension_semantics=("arbitrary",) * n_axes, vmem_limit_bytes=VMEM_LIMIT)


def _dot(a, b):
    return jnp.dot(a, b, preferred_element_type=f32)


def _dot_nt(a, b):
    return lax.dot_general(a, b, (((1,), (1,)), ((), ())), preferred_element_type=f32)


def _split(x):
    hi = x.astype(bf16)
    lo = (x - hi.astype(f32)).astype(bf16)
    return hi, lo


def _silu(x):
    return x * jax.nn.sigmoid(x)


def _softplus(x):
    return jnp.maximum(x, 0.0) + jnp.log1p(jnp.exp(-jnp.abs(x)))


def _modulated_norm(x, norm_w, shift, scale):
    ms = jnp.mean(x * x, axis=-1, keepdims=True)
    return (x * lax.rsqrt(ms + EPS) * norm_w) * (1.0 + scale) + shift


def _ada_mod_kernel(cond_ref, w_ref, b_ref, o_ref):
    s = _silu(cond_ref[...])
    sh, sl = _split(s)
    wh, wl = _split(w_ref[...])
    o_ref[...] = _dot(sh, wh) + (_dot(sh, wl) + _dot(sl, wh)) + b_ref[...]


def _ada_mod(cond, mod_w, mod_b):
    rows = cond.shape[0]
    tn = 1024
    return pl.pallas_call(
        _ada_mod_kernel,
        grid=(3 * D_MODEL // tn,),
        in_specs=[pl.BlockSpec((rows, D_MODEL), lambda j: (0, 0)),
                  pl.BlockSpec((D_MODEL, tn), lambda j: (0, j)),
                  pl.BlockSpec((1, tn), lambda j: (0, j))],
        out_specs=pl.BlockSpec((rows, tn), lambda j: (0, j)),
        out_shape=jax.ShapeDtypeStruct((rows, 3 * D_MODEL), f32),
        compiler_params=_params(1),
        name="ada_mod",
    )(cond, mod_w, mod_b.reshape(1, -1))


def _head_rms(t, e, et, w):
    hi, lo = _split(t * t)
    ss = _dot(hi, e) + _dot(lo, e)
    rh, rl = _split(lax.rsqrt(ss * (1.0 / HEAD_DIM) + EPS))
    return t * (_dot(rh, et) + _dot(rl, et)) * w


def _rope(t, cos, sin_a, sin_b):
    width = t.shape[1]
    rep = width // LANES
    tile = lambda u: jnp.concatenate([u] * rep, axis=1)
    return (t * tile(cos) + pltpu.roll(t, width - AXIS_DIM // 2, 1) * tile(sin_a)
            + pltpu.roll(t, AXIS_DIM // 2, 1) * tile(sin_b))


def _l0_inproj_kernel(*refs, latent):
    x_ref, mod_ref, nw_ref, w_ref, qn_ref, kn_ref, e_ref, et_ref = refs[:8]
    refs = refs[8:]
    if latent:
        cos_ref, sa_ref, sb_ref = refs[:3]
        q_ref, kk_ref, vt_ref, gs_ref = refs[3:]
    else:
        q_ref, kk_ref, vt_ref, gs_ref, knew_ref, vnew_ref = refs
    mod = mod_ref[0]
    e = e_ref[...]
    et = et_ref[...]
    subs = [slice(r0, r0 + L0_SUB_ROWS) for r0 in range(0, x_ref.shape[1], L0_SUB_ROWS)]
    projs = []
    for rows in subs:
        h = _modulated_norm(x_ref[0, rows, :], nw_ref[...], mod[:, :D_MODEL], mod[:, D_MODEL:2 * D_MODEL])
        projs.append(_dot(h.astype(bf16), w_ref[...]))
    low = lax.broadcasted_iota(jnp.int32, (L0_SUB_ROWS, LANES), 1) < HEAD_DIM
    ones = jnp.ones((VT_ROWS - HEAD_DIM, L0_SUB_ROWS), bf16)
    for rows, proj in zip(subs, projs):
        q = proj[:, :ATTN_WIDTH]
        k = proj[:, ATTN_WIDTH:ATTN_WIDTH + KV_WIDTH]
        v = proj[:, ATTN_WIDTH + KV_WIDTH:ATTN_WIDTH + 2 * KV_WIDTH]
        g = proj[:, ATTN_WIDTH + 2 * KV_WIDTH:]
        q = _head_rms(q, e, et, qn_ref[...])
        k = _head_rms(k, e[:KV_WIDTH], et[:, :KV_WIDTH], kn_ref[...])
        if latent:
            cos, sa, sb = cos_ref[rows, :], sa_ref[rows, :], sb_ref[rows, :]
            q = _rope(q, cos, sa, sb)
            k = _rope(k, cos, sa, sb)
        else:
            knew_ref[0, rows, :] = k
            vnew_ref[0, rows, :] = v
        q_ref[0, rows, :] = q.astype(bf16)
        gs_ref[0, rows, :] = _silu(g).astype(bf16)
        v_t = v.T
        for j in range(KV_WIDTH // LANES):
            kb = k[:, LANES * j:LANES * (j + 1)]
            kbs = pltpu.roll(kb, HEAD_DIM, 1)
            kk_ref[0, 2 * j, rows, :] = jnp.where(low, kb, kbs).astype(bf16)
            kk_ref[0, 2 * j + 1, rows, :] = jnp.where(low, kbs, kb).astype(bf16)
        for kvh in range(N_KV_HEADS):
            vt_ref[0, kvh, :HEAD_DIM, rows] = v_t[HEAD_DIM * kvh:HEAD_DIM * (kvh + 1)].astype(bf16)
            vt_ref[0, kvh, HEAD_DIM:, rows] = ones


def _l0_inproj(x, mod, norm_w, w_in, qn, kn, e, et, rope_tabs, *, latent, tm):
    b, t, _ = x.shape
    attn_in = w_in.shape[1]
    row = (lambda bi, i: (bi + 1, 0, 0)) if latent else (lambda bi, i: (0, 0, 0))
    const = lambda bi, i: (0, 0)
    in_specs = [pl.BlockSpec((1, tm, D_MODEL), lambda bi, i: (bi, i, 0)),
                pl.BlockSpec((1, 1, 3 * D_MODEL), row),
                pl.BlockSpec((1, D_MODEL), const),
                pl.BlockSpec((D_MODEL, attn_in), const),
                pl.BlockSpec((1, ATTN_WIDTH), const),
                pl.BlockSpec((1, KV_WIDTH), const),
                pl.BlockSpec((ATTN_WIDTH, LANES), const),
                pl.BlockSpec((LANES, ATTN_WIDTH), const)]
    args = [x, mod, norm_w, w_in, qn, kn, e, et]
    tok = lambda bi, i: (bi, i, 0)
    out_specs = [pl.BlockSpec((1, tm, ATTN_WIDTH), tok),
                 pl.BlockSpec((1, N_KV_HEADS, tm, LANES), lambda bi, i: (bi, 0, i, 0)),
                 pl.BlockSpec((1, N_KV_HEADS, VT_ROWS, tm), lambda bi, i: (bi, 0, 0, i)),
                 pl.BlockSpec((1, tm, ATTN_WIDTH), tok)]
    out_shape = [jax.ShapeDtypeStruct((b, t, ATTN_WIDTH), bf16),
                 jax.ShapeDtypeStruct((b, N_KV_HEADS, t, LANES), bf16),
                 jax.ShapeDtypeStruct((b, N_KV_HEADS, VT_ROWS, t), bf16),
                 jax.ShapeDtypeStruct((b, t, ATTN_WIDTH), bf16)]
    if latent:
        in_specs += [pl.BlockSpec((tm, LANES), lambda bi, i: (i, 0))] * 3
        args += list(rope_tabs)
    else:
        out_specs += [pl.BlockSpec((1, tm, KV_WIDTH), tok)] * 2
        out_shape += [jax.ShapeDtypeStruct((b, t, KV_WIDTH), f32)] * 2
    return pl.pallas_call(
        functools.partial(_l0_inproj_kernel, latent=latent),
        grid=(b, t // tm),
        in_specs=in_specs, out_specs=out_specs, out_shape=out_shape,
        compiler_params=_params(2),
        name="l0_inproj_latent" if latent else "l0_inproj_context",
    )(*args)


def _attn_kernel(*refs, n_src):
    q_ref = refs[0]
    kv_refs = refs[1:1 + 2 * n_src]
    o_ref = refs[-1]
    n_sub = q_ref.shape[1] // Q_SUB_ROWS
    low = lax.broadcasted_iota(jnp.int32, (Q_SUB_ROWS, LANES), 1) < HEAD_DIM
    zero = jnp.zeros((Q_SUB_ROWS, LANES), bf16)
    chunks = []
    for i in range(n_src):
        n_keys = kv_refs[2 * i].shape[2]
        width = min(KEY_CHUNK, n_keys)
        chunks += [(kv_refs[2 * i], kv_refs[2 * i + 1], slice(c0, c0 + width))
                   for c0 in range(0, n_keys, width)]
    steps = [(sub, head, ch) for sub in range(n_sub) for head in range(4)
             for ch in range(len(chunks))]

    def q_masked(sub, head):
        qp = q_ref[0, Q_SUB_ROWS * sub:Q_SUB_ROWS * (sub + 1),
                   LANES * (head // 2):LANES * (head // 2 + 1)]
        return jnp.where(low, qp, zero) if head % 2 == 0 else jnp.where(low, zero, qp)

    def scores(step):
        sub, head, ch = step
        k_ref, _, keys = chunks[ch]
        return _dot_nt(k_ref[0, 0, keys, :], q_masked(sub, head))

    halves = []
    m = acc = None
    queue = [scores(st) for st in steps[:SCORE_LOOKAHEAD]]
    for idx, (sub, head, ch) in enumerate(steps):
        s = queue.pop(0)
        if idx + SCORE_LOOKAHEAD < len(steps):
            queue.append(scores(steps[idx + SCORE_LOOKAHEAD]))
        _, vt_ref, keys = chunks[ch]
        slab = s.shape[0] // 4
        m_c = jnp.maximum(jnp.maximum(s[:slab], s[slab:2 * slab]),
                          jnp.maximum(s[2 * slab:3 * slab], s[3 * slab:]))
        m_c = jnp.max(m_c, axis=0, keepdims=True)
        m_new = m_c if ch == 0 else jnp.maximum(m, m_c)
        pv = _dot(vt_ref[0, 0, :, keys], jnp.exp2(s - m_new).astype(bf16))
        acc = pv if ch == 0 else acc * jnp.exp2(m - m_new) + pv
        m = m_new
        if ch == len(chunks) - 1:
            halves.append(acc[:HEAD_DIM] / acc[HEAD_DIM:HEAD_DIM + 1])
            if head % 2 == 1:
                o_ref[0, Q_SUB_ROWS * sub:Q_SUB_ROWS * (sub + 1),
                      LANES * (head // 2):LANES * (head // 2 + 1)] = (
                    jnp.concatenate(halves, axis=0).T.astype(bf16))
                halves = []


def _attention(q, kv_sources, *, tq):
    b, t, _ = q.shape
    in_specs = [pl.BlockSpec((1, tq, 2 * LANES), lambda bi, g, i: (bi, i, g))]
    args = [q]
    for arr in kv_sources:
        in_specs.append(pl.BlockSpec((1, 1) + arr.shape[2:], lambda bi, g, i: (bi, g, 0, 0)))
        args.append(arr)
    return pl.pallas_call(
        functools.partial(_attn_kernel, n_src=len(kv_sources) // 2),
        grid=(b, N_KV_HEADS, t // tq),
        in_specs=in_specs,
        out_specs=pl.BlockSpec((1, tq, 2 * LANES), lambda bi, g, i: (bi, i, g)),
        out_shape=jax.ShapeDtypeStruct((b, t, ATTN_WIDTH), bf16),
        compiler_params=_params(3),
        name="attention_latent" if len(kv_sources) > 2 else "attention_context",
    )(*args)


def _mid_kernel(x_ref, xp_ref, xn_ref, o_ref, op_ref, on_ref, g_ref, gp_ref, gn_ref,
                mod0_ref, mod1_ref, wout_ref, nw_ref, wzx_ref, wdt_ref, cw_ref, cb_ref, dtb_ref,
                alog_ref, lmat_ref, umat_ref,
                x1_ref, z_ref, xs_ref, bc_ref, acs_ref, rowt_ref, w_ref):
    i = pl.program_id(1)
    last = pl.num_programs(1) - 1
    tm = x_ref.shape[1]
    rows = tm + 2 * HALO
    body = slice(HALO, HALO + tm)
    cat = lambda a, b_, c: jnp.concatenate([a[0], b_[0], c[0]], axis=0)
    gated = cat(op_ref, o_ref, on_ref) * cat(gp_ref, g_ref, gn_ref)
    x1 = cat(xp_ref, x_ref, xn_ref) + mod0_ref[0][:, 2 * D_MODEL:] * _dot(gated, wout_ref[...])
    x1_ref[0] = x1[body]
    mod = mod1_ref[0]
    h = _modulated_norm(x1, nw_ref[...], mod[:, :D_MODEL], mod[:, D_MODEL:2 * D_MODEL])
    h_body = h[body].astype(bf16)
    h = jnp.concatenate([jnp.where(i == 0, 0.0, h[:HALO]).astype(bf16), h_body,
                         jnp.where(i == last, 0.0, h[HALO + tm:]).astype(bf16)], axis=0)

    n_z = D_INNER // PROJ_BLOCK
    tasks = []
    for j in range(CONV_DIM // PROJ_BLOCK):
        tasks.append(("conv", j))
        if j < n_z:
            tasks.append(("z", j))

    def project(task):
        kind, j = task
        if kind == "z":
            return _dot(h_body, wzx_ref[:, PROJ_BLOCK * j:PROJ_BLOCK * (j + 1)])
        return _dot(h, wzx_ref[:, D_INNER + PROJ_BLOCK * j:D_INNER + PROJ_BLOCK * (j + 1)])

    queue = [project(t) for t in tasks[:PROJ_LOOKAHEAD]]
    for idx, (kind, j) in enumerate(tasks):
        blk = queue.pop(0)
        if idx + PROJ_LOOKAHEAD < len(tasks):
            queue.append(project(tasks[idx + PROJ_LOOKAHEAD]))
        cols = slice(PROJ_BLOCK * j, PROJ_BLOCK * (j + 1))
        if kind == "z":
            z_ref[0, :, cols] = blk.astype(bf16)
            continue
        act = _silu(cw_ref[0:1, cols] * pltpu.roll(blk, 1, 0)[body] + cw_ref[1:2, cols] * blk[body]
                    + cw_ref[2:3, cols] * pltpu.roll(blk, rows - 1, 0)[body]
                    + cb_ref[:, cols]).astype(bf16)
        if j < n_z:
            xs_ref[0, :, cols] = act
        else:
            bc_ref[0, :, PROJ_BLOCK * (j - n_z):PROJ_BLOCK * (j - n_z + 1)] = act

    dt = _softplus(_dot(h_body, wdt_ref[...]) + dtb_ref[...])
    a = -jnp.exp(alog_ref[...])
    fwd_lane = lax.broadcasted_iota(jnp.int32, (CHUNK, LANES), 1) < SSD_HEADS
    lower, upper = lmat_ref[...], umat_ref[...]
    for c in range(tm // CHUNK):
        dtc = dt[CHUNK * c:CHUNK * (c + 1)]
        ah, al = _split(dtc * a)
        acs = jnp.where(fwd_lane, _dot(lower, ah) + _dot(lower, al), _dot(upper, ah) + _dot(upper, al))
        end_row = jnp.where(fwd_lane[0:1], acs[CHUNK - 1:CHUNK], acs[0:1])
        acs_ref[0, CHUNK * c:CHUNK * (c + 1)] = acs
        w_ref[0, CHUNK * c:CHUNK * (c + 1)] = dtc * jnp.exp(end_row - acs)
        rowt_ref[0, c] = (acs - jnp.log(dtc)).T


def _mid(x, o, gs, mod0, mod1, w_out0, norm_w, w_zx, w_dt, conv_w, conv_b, dt_bias, alog, lmat, umat,
         *, latent, tm):
    b, t, _ = x.shape
    row = (lambda bi, i: (bi + 1, 0, 0)) if latent else (lambda bi, i: (0, 0, 0))
    const = lambda bi, i: (0, 0)
    tok = lambda bi, i: (bi, i, 0)
    hb = tm // HALO
    n_hb = t // HALO
    prev_map = lambda bi, i: (bi, jnp.maximum(i * hb - 1, 0), 0)
    next_map = lambda bi, i: (bi, jnp.minimum((i + 1) * hb, n_hb - 1), 0)
    halo3 = lambda width: [pl.BlockSpec((1, tm, width), tok),
                           pl.BlockSpec((1, HALO, width), prev_map),
                           pl.BlockSpec((1, HALO, width), next_map)]
    resident = lambda shape: pl.BlockSpec(shape, const, pipeline_mode=pl.Buffered(1))
    chunk4 = lambda bi, i: (bi, i, 0, 0)
    ncs = tm // CHUNK
    return pl.pallas_call(
        _mid_kernel,
        grid=(b, t // tm),
        in_specs=halo3(D_MODEL) + halo3(ATTN_WIDTH) + halo3(ATTN_WIDTH) + [
            pl.BlockSpec((1, 1, 3 * D_MODEL), row),
            pl.BlockSpec((1, 1, 3 * D_MODEL), row),
            resident((ATTN_WIDTH, D_MODEL)),
            pl.BlockSpec((1, D_MODEL), const),
            resident((D_MODEL, D_INNER + CONV_DIM)),
            resident((D_MODEL, LANES)),
            pl.BlockSpec((3, CONV_DIM), const),
            pl.BlockSpec((1, CONV_DIM), const),
            pl.BlockSpec((1, LANES), const),
            pl.BlockSpec((1, LANES), const),
            pl.BlockSpec((CHUNK, CHUNK), const),
            pl.BlockSpec((CHUNK, CHUNK), const)],
        out_specs=[pl.BlockSpec((1, tm, D_MODEL), tok),
                   pl.BlockSpec((1, tm, D_INNER), tok),
                   pl.BlockSpec((1, tm, D_INNER), tok),
                   pl.BlockSpec((1, tm, BC_WIDTH), tok),
                   pl.BlockSpec((1, tm, LANES), tok),
                   pl.BlockSpec((1, ncs, LANES, CHUNK), chunk4),
                   pl.BlockSpec((1, tm, LANES), tok)],
        out_shape=[jax.ShapeDtypeStruct((b, t, D_MODEL), f32),
                   jax.ShapeDtypeStruct((b, t, D_INNER), bf16),
                   jax.ShapeDtypeStruct((b, t, D_INNER), bf16),
                   jax.ShapeDtypeStruct((b, t, BC_WIDTH), bf16),
                   jax.ShapeDtypeStruct((b, t, LANES), f32),
                   jax.ShapeDtypeStruct((b, t // CHUNK, LANES, CHUNK), f32),
                   jax.ShapeDtypeStruct((b, t, LANES), f32)],
        compiler_params=_params(2),
        name="mid_latent" if latent else "mid_context",
    )(x, x, x, o, o, o, gs, gs, gs, mod0, mod1, w_out0, norm_w, w_zx, w_dt, conv_w, conv_b,
      dt_bias, alog, lmat, umat)


def _ssd_kernel(*refs, backward, has_h0, final_state):
    xs_ref, bc_ref, acs_ref, rowt_ref, w_ref, e_ref = refs[:6]
    refs = refs[6:]
    if has_h0:
        h0_ref, refs = refs[0], refs[1:]
    if backward:
        yf_ref, z_ref, x1_ref, mod_ref, gw_ref, wout_ref, dskip_ref = refs[:7]
        refs = refs[7:]
    out_ref, refs = refs[0], refs[1:]
    if final_state:
        hfin_ref, refs = refs[0], refs[1:]
    ht_ref = refs[0]
    if backward:
        yb_ref, yn_ref = refs[1:3]

    step = pl.program_id(1)
    n_pairs = SSD_HEADS // 2
    pairs_per_group = HEADS_PER_GROUP // 2
    group_w = HEADS_PER_GROUP * SSD_HEAD_DIM
    n_chunks = xs_ref.shape[1] // CHUNK

    @pl.when(step == 0)
    def _init():
        if has_h0:
            for j in range(n_pairs):
                ht_ref[:, LANES * j:LANES * (j + 1)] = h0_ref[0, LANES * j:LANES * (j + 1), :].T
        else:
            ht_ref[...] = jnp.zeros_like(ht_ref)

    h_off = SSD_HEADS if backward else 0
    end = 0 if backward else CHUNK - 1
    ri = lax.broadcasted_iota(jnp.int32, (CHUNK, CHUNK), 0)
    ci = lax.broadcasted_iota(jnp.int32, (CHUNK, CHUNK), 1)
    mask = (ri <= ci) if backward else (ri >= ci)
    low = lax.broadcasted_iota(jnp.int32, (CHUNK, LANES), 1) < SSD_HEAD_DIM

    chunk_order = list(range(n_chunks - 1, -1, -1) if backward else range(n_chunks))

    cbs, xws = {}, {}
    for k in chunk_order:
        rows = slice(CHUNK * k, CHUNK * (k + 1))
        w_b = w_ref[0, rows, :].astype(bf16)
        for g in range(SSD_GROUPS):
            gcols = slice(group_w * g, group_w * (g + 1))
            bm = bc_ref[0, rows, D_STATE * g:D_STATE * (g + 1)]
            cm = bc_ref[0, rows, D_STATE * (SSD_GROUPS + g):D_STATE * (SSD_GROUPS + g + 1)]
            cbs[k, g] = _dot_nt(cm, bm)
            xws[k, g] = xs_ref[0, rows, gcols] * _dot(w_b, e_ref[:, gcols]).astype(bf16)

    for k in chunk_order:
        rows = slice(CHUNK * k, CHUNK * (k + 1))
        acs = acs_ref[0, rows, :]
        rowt = rowt_ref[0, k][h_off:h_off + SSD_HEADS]
        cd_h, cd_l = _split(jnp.broadcast_to(jnp.exp(acs[end:end + 1, :]), (HALF_VREG_ROWS, LANES)))
        chunk_decay = (_dot(cd_h, e_ref[...]) + _dot(cd_l, e_ref[...]))[0:1]
        y_dst = yb_ref if backward else out_ref.at[0, rows]

        for g in range(SSD_GROUPS):
            bm = bc_ref[0, rows, D_STATE * g:D_STATE * (g + 1)]
            cm_f = bc_ref[0, rows, D_STATE * (SSD_GROUPS + g):D_STATE * (SSD_GROUPS + g + 1)].astype(f32)
            cb = cbs[k, g]
            for jp in range(pairs_per_group):
                j = g * pairs_per_group + jp
                cols = slice(LANES * j, LANES * (j + 1))
                rhs = jnp.concatenate([xs_ref[0, rows, cols], ht_ref[:, cols].astype(bf16)], axis=0)
                ys = []
                for par in range(2):
                    hl = 2 * j + par
                    col = jnp.broadcast_to(acs[:, h_off + hl:h_off + hl + 1], (CHUNK, CHUNK))
                    decay = jnp.exp(jnp.where(mask, col - rowt[hl:hl + 1, :], NEG_BIG))
                    m_diag = (cb * decay).astype(bf16)
                    c_off = (cm_f * jnp.exp(col)).astype(bf16)
                    ys.append(_dot(jnp.concatenate([m_diag, c_off], axis=1), rhs))
                y_dst[:, cols] = jnp.where(low, ys[0], ys[1])
            gcols = slice(group_w * g, group_w * (g + 1))
            st = _dot(bm.astype(f32).T.astype(bf16), xws[k, g])
            ht_ref[:, gcols] = ht_ref[:, gcols] * chunk_decay[:, gcols] + st

        if backward:
            y = yf_ref[0, rows, :] + yb_ref[...] + dskip_ref[...] * xs_ref[0, rows, :].astype(f32)
            yz = y * _silu(z_ref[0, rows, :].astype(f32))
            ms = jnp.mean(yz * yz, axis=-1, keepdims=True)
            yn_ref[rows, :] = (yz * lax.rsqrt(ms + EPS) * gw_ref[...]).astype(bf16)

    if backward:
        gate = mod_ref[0][:, 2 * D_MODEL:]
        out_ref[0] = x1_ref[0] + gate * _dot(yn_ref[...], wout_ref[...])

    if final_state:
        @pl.when(step == pl.num_programs(1) - 1)
        def _fin():
            for j in range(n_pairs):
                hfin_ref[0, LANES * j:LANES * (j + 1), :] = ht_ref[:, LANES * j:LANES * (j + 1)].T


def _ssd_pass(xs, bc, acs, rowt, w, e_dir, h0, extra, *, backward, latent):
    b, t, _ = xs.shape
    rows = SSD_STEP_CHUNKS * CHUNK
    n_steps = t // rows
    has_h0 = h0 is not None
    final_state = not latent
    blk = (lambda bi, c: (bi, n_steps - 1 - c, 0)) if backward else (lambda bi, c: (bi, c, 0))
    blk4 = (lambda bi, c: (bi, n_steps - 1 - c, 0, 0)) if backward else (lambda bi, c: (bi, c, 0, 0))
    const = lambda bi, c: (0, 0)
    per_b = lambda bi, c: (bi, 0, 0)
    row = (lambda bi, c: (bi + 1, 0, 0)) if latent else (lambda bi, c: (0, 0, 0))
    in_specs = [pl.BlockSpec((1, rows, D_INNER), blk),
                pl.BlockSpec((1, rows, BC_WIDTH), blk),
                pl.BlockSpec((1, rows, LANES), blk),
                pl.BlockSpec((1, SSD_STEP_CHUNKS, LANES, CHUNK), blk4),
                pl.BlockSpec((1, rows, LANES), blk),
                pl.BlockSpec((LANES, D_INNER), const)]
    args = [xs, bc, acs, rowt, w, e_dir]
    if has_h0:
        in_specs.append(pl.BlockSpec((1, D_INNER, D_STATE), per_b))
        args.append(h0)
    scratch = [pltpu.VMEM((D_STATE, D_INNER), f32)]
    if backward:
        yf, z, x1, mod, gw, wout, dskip = extra
        in_specs += [pl.BlockSpec((1, rows, D_INNER), blk),
                     pl.BlockSpec((1, rows, D_INNER), blk),
                     pl.BlockSpec((1, rows, D_MODEL), blk),
                     pl.BlockSpec((1, 1, 3 * D_MODEL), row),
                     pl.BlockSpec((1, D_INNER), const),
                     pl.BlockSpec((D_INNER, D_MODEL), const),
                     pl.BlockSpec((1, D_INNER), const)]
        args += [yf, z, x1, mod, gw, wout, dskip]
        out_specs = [pl.BlockSpec((1, rows, D_MODEL), blk)]
        out_shape = [jax.ShapeDtypeStruct((b, t, D_MODEL), f32)]
        scratch += [pltpu.VMEM((CHUNK, D_INNER), f32), pltpu.VMEM((rows, D_INNER), bf16)]
    else:
        out_specs = [pl.BlockSpec((1, rows, D_INNER), blk)]
        out_shape = [jax.ShapeDtypeStruct((b, t, D_INNER), f32)]
    if final_state:
        out_specs.append(pl.BlockSpec((1, D_INNER, D_STATE), per_b))
        out_shape.append(jax.ShapeDtypeStruct((b, D_INNER, D_STATE), f32))
    return pl.pallas_call(
        functools.partial(_ssd_kernel, backward=backward, has_h0=has_h0, final_state=final_state),
        grid=(b, n_steps),
        in_specs=in_specs, out_specs=out_specs, out_shape=out_shape,
        scratch_shapes=scratch,
        compiler_params=_params(2),
        name="ssd_bwd" if backward else "ssd_fwd",
    )(*args)


def _rope_tables(n_tokens):
    rows = n_tokens // GRID_W
    row_ids = jnp.repeat(jnp.arange(rows), GRID_W).astype(f32)
    col_ids = jnp.tile(jnp.arange(GRID_W), rows).astype(f32)
    inv_freq = 1.0 / (ROPE_THETA ** (jnp.arange(0, AXIS_DIM, 2, dtype=f32) / AXIS_DIM))
    ang = jnp.stack([row_ids[:, None] * inv_freq, col_ids[:, None] * inv_freq], axis=1)
    cos, sin, zero = jnp.cos(ang), jnp.sin(ang), jnp.zeros_like(ang)
    head = lambda a, b_: jnp.stack([a, b_], axis=2).reshape(n_tokens, HEAD_DIM)
    two = lambda u: jnp.concatenate([u, u], axis=1)
    return two(head(cos, cos)), two(head(-sin, zero)), two(head(zero, sin))


def kernel(x_prompt, x_sample, cache_k_l0, cache_v_l0, state_fwd_l1, state_bwd_l1, c, c_ctx,
           l0_norm_w, l0_mod_w, l0_mod_b, l0_w_in, l0_q_norm, l0_k_norm, l0_w_out,
           l1_norm_w, l1_mod_w, l1_mod_b, l1_w_in, l1_conv_w, l1_conv_b, l1_dt_bias_f, l1_dt_bias_b,
           l1_a_log_f, l1_a_log_b, l1_d_skip, l1_gnorm_w, l1_w_out):
    dec_b = x_sample.shape[0]
    pad = lambda u, n: jnp.pad(u, (0, n - u.shape[0]))

    n_rows = 16
    cond = jnp.zeros((n_rows, D_MODEL), f32).at[0].set(c_ctx).at[1:1 + dec_b].set(c)
    mod0 = _ada_mod(cond, l0_mod_w, l0_mod_b).reshape(n_rows, 1, 3 * D_MODEL)
    mod1 = _ada_mod(cond, l1_mod_w, l1_mod_b).reshape(n_rows, 1, 3 * D_MODEL)

    q_scale = HEAD_DIM ** -0.5 * math.log2(math.e)
    qn = (jnp.tile(l0_q_norm, N_HEADS) * q_scale).reshape(1, ATTN_WIDTH)
    kn = jnp.tile(l0_k_norm, N_KV_HEADS).reshape(1, KV_WIDTH)
    head_of = jnp.arange(ATTN_WIDTH) // HEAD_DIM
    e = (head_of[:, None] == jnp.arange(LANES)[None, :]).astype(bf16)
    et = e.T
    w_in0 = l0_w_in.astype(bf16)
    w_out0 = l0_w_out.astype(bf16)
    w_zx = l1_w_in[:, :D_INNER + CONV_DIM].astype(bf16)
    w_dt = jnp.pad(l1_w_in[:, D_INNER + CONV_DIM:], ((0, 0), (0, LANES - 2 * SSD_HEADS))).astype(bf16)
    w_out1 = l1_w_out.astype(bf16)
    dt_bias = pad(jnp.concatenate([l1_dt_bias_f, l1_dt_bias_b]), LANES).reshape(1, LANES)
    alog = pad(jnp.concatenate([l1_a_log_f, l1_a_log_b]), LANES).reshape(1, LANES)
    dskip = jnp.repeat(l1_d_skip, SSD_HEAD_DIM).reshape(1, D_INNER)
    nw0 = l0_norm_w.reshape(1, D_MODEL)
    nw1 = l1_norm_w.reshape(1, D_MODEL)
    gw = l1_gnorm_w.reshape(1, D_INNER)
    conv_b = l1_conv_b.reshape(1, CONV_DIM)
    tri = jnp.arange(CHUNK)
    lmat = (tri[None, :] <= tri[:, None]).astype(bf16)
    umat = lmat.T
    rope_tabs = _rope_tables(x_sample.shape[1])
    ssd_head_of = jnp.arange(D_INNER) // SSD_HEAD_DIM
    e_fwd = (jnp.arange(LANES)[:, None] == ssd_head_of[None, :]).astype(bf16)
    e_bwd = (jnp.arange(LANES)[:, None] == SSD_HEADS + ssd_head_of[None, :]).astype(bf16)

    kc = cache_k_l0.transpose(0, 2, 1, 3).astype(bf16)
    vc = cache_v_l0.transpose(0, 2, 3, 1).astype(bf16)
    kk_ctx = jnp.concatenate([kc, kc], axis=-1)
    vt_ctx = jnp.concatenate(
        [vc, jnp.ones(vc.shape[:2] + (VT_ROWS - HEAD_DIM, vc.shape[3]), bf16)], axis=2)
    h0_f = state_fwd_l1.reshape(dec_b, D_INNER, D_STATE)
    h0_b = state_bwd_l1.reshape(dec_b, D_INNER, D_STATE)

    outs = {}
    for latent, x in ((False, x_prompt), (True, x_sample)):
        res = _l0_inproj(x, mod0, nw0, w_in0, qn, kn, e, et, rope_tabs, latent=latent,
                         tm=2 * L0_SUB_ROWS if latent else L0_SUB_ROWS)
        if latent:
            q, kk, vt, gs = res
            kv = [kk_ctx, vt_ctx, kk, vt]
        else:
            q, kk, vt, gs, k_new, v_new = res
            kv = [kk, vt]
        o = _attention(q, kv, tq=4 * Q_SUB_ROWS if latent else Q_SUB_ROWS)
        x1, z, xs, bc, acs, rowt, w = _mid(
            x, o, gs, mod0, mod1, w_out0, nw1, w_zx, w_dt, l1_conv_w, conv_b, dt_bias, alog,
            lmat, umat, latent=latent, tm=512 if latent else 256)
        fwd = _ssd_pass(xs, bc, acs, rowt, w, e_fwd, h0_f if latent else None, None,
                        backward=False, latent=latent)
        bwd = _ssd_pass(xs, bc, acs, rowt, w, e_bwd, h0_b if latent else None,
                        (fwd[0], z, x1, mod1, gw, w_out1, dskip), backward=True, latent=latent)
        if latent:
            outs["y_sample"] = bwd[0]
        else:
            b = x.shape[0]
            outs["y_prompt"] = bwd[0]
            outs["k"] = k_new.reshape(b, -1, N_KV_HEADS, HEAD_DIM)
            outs["v"] = v_new.reshape(b, -1, N_KV_HEADS, HEAD_DIM)
            outs["hf"] = fwd[1].reshape(b, SSD_HEADS, SSD_HEAD_DIM, D_STATE)
            outs["hb"] = bwd[1].reshape(b, SSD_HEADS, SSD_HEAD_DIM, D_STATE)
    return (outs["y_prompt"], outs["y_sample"], outs["k"], outs["v"], outs["hf"], outs["hb"])
```

```python
import functools
import math

import jax
import jax.numpy as jnp
from jax import lax
from jax.experimental import pallas as pl
from jax.experimental.pallas import tpu as pltpu

f32 = jnp.float32
bf16 = jnp.bfloat16

D_MODEL = 1024
GRID_W = 64
EPS = 1e-6
N_HEADS = 16
N_KV_HEADS = 4
HEAD_DIM = 64
ATTN_WIDTH = N_HEADS * HEAD_DIM
KV_WIDTH = N_KV_HEADS * HEAD_DIM
AXIS_DIM = HEAD_DIM // 2
ROPE_THETA = 10000.0
D_INNER = 2048
SSD_HEAD_DIM = 64
SSD_HEADS = D_INNER // SSD_HEAD_DIM
SSD_GROUPS = 4
HEADS_PER_GROUP = SSD_HEADS // SSD_GROUPS
D_STATE = 128
CHUNK = 128
BC_WIDTH = 2 * SSD_GROUPS * D_STATE
CONV_DIM = D_INNER + BC_WIDTH

LANES = 128
BF16_ROWS = 16
HALO = BF16_ROWS
PROJ_BLOCK = 256
PROJ_LOOKAHEAD = 2
L0_SUB_ROWS = 256
SSD_STEP_CHUNKS = 2
HALF_VREG_ROWS = 8
GROUP_W = HEADS_PER_GROUP * SSD_HEAD_DIM
VT_ROWS = HEAD_DIM + BF16_ROWS
Q_SUB_ROWS = 256
KEY_CHUNK = 256
SCORE_LOOKAHEAD = 8
VMEM_LIMIT = 56 * 1024 * 1024

NEG_BIG = -1e30


def _params(n_axes):
    return pltpu.CompilerParams(
        dimension_semantics=("arbitrary",) * n_axes, vmem_limit_bytes=VMEM_LIMIT)


def _dot(a, b):
    return jnp.dot(a, b, preferred_element_type=f32)


def _dot_nt(a, b):
    return lax.dot_general(a, b, (((1,), (1,)), ((), ())), preferred_element_type=f32)


def _split(x):
    hi = x.astype(bf16)
    lo = (x - hi.astype(f32)).astype(bf16)
    return hi, lo


def _silu(x):
    return x * jax.nn.sigmoid(x)


def _softplus(x):
    return jnp.maximum(x, 0.0) + jnp.log1p(jnp.exp(-jnp.abs(x)))


def _modulated_norm(x, norm_w, shift, scale):
    ms = jnp.mean(x * x, axis=-1, keepdims=True)
    return (x * lax.rsqrt(ms + EPS) * norm_w) * (1.0 + scale) + shift


def _ada_mod_kernel(cond_ref, w_ref, b_ref, o_ref):
    s = _silu(cond_ref[...])
    sh, sl = _split(s)
    wh, wl = _split(w_ref[...])
    o_ref[...] = _dot(sh, wh) + (_dot(sh, wl) + _dot(sl, wh)) + b_ref[...]


def _ada_mod(cond, mod_w, mod_b):
    rows = cond.shape[0]
    tn = 1024
    return pl.pallas_call(
        _ada_mod_kernel,
        grid=(3 * D_MODEL // tn,),
        in_specs=[pl.BlockSpec((rows, D_MODEL), lambda j: (0, 0)),
                  pl.BlockSpec((D_MODEL, tn), lambda j: (0, j)),
                  pl.BlockSpec((1, tn), lambda j: (0, j))],
        out_specs=pl.BlockSpec((rows, tn), lambda j: (0, j)),
        out_shape=jax.ShapeDtypeStruct((rows, 3 * D_MODEL), f32),
        compiler_params=_params(1),
        name="ada_mod",
    )(cond, mod_w, mod_b.reshape(1, -1))


def _head_rms(t, e, et, w):
    hi, lo = _split(t * t)
    ss = _dot(hi, e) + _dot(lo, e)
    rh, rl = _split(lax.rsqrt(ss * (1.0 / HEAD_DIM) + EPS))
    return t * (_dot(rh, et) + _dot(rl, et)) * w


def _rope(t, cos, sin_a, sin_b):
    width = t.shape[1]
    rep = width // LANES
    tile = lambda u: jnp.concatenate([u] * rep, axis=1)
    return (t * tile(cos) + pltpu.roll(t, width - AXIS_DIM // 2, 1) * tile(sin_a)
            + pltpu.roll(t, AXIS_DIM // 2, 1) * tile(sin_b))


def _l0_inproj_kernel(*refs, latent):
    x_ref, mod_ref, nw_ref, w_ref, qn_ref, kn_ref, e_ref, et_ref = refs[:8]
    refs = refs[8:]
    if latent:
        cos_ref, sa_ref, sb_ref = refs[:3]
        q_ref, kk_ref, vt_ref, gs_ref = refs[3:]
    else:
        q_ref, kk_ref, vt_ref, gs_ref, knew_ref, vnew_ref = refs
    mod = mod_ref[0]
    e = e_ref[...]
    et = et_ref[...]
    subs = [slice(r0, r0 + L0_SUB_ROWS) for r0 in range(0, x_ref.shape[1], L0_SUB_ROWS)]
    projs = []
    for rows in subs:
        h = _modulated_norm(x_ref[0, rows, :], nw_ref[...], mod[:, :D_MODEL], mod[:, D_MODEL:2 * D_MODEL])
        projs.append(_dot(h.astype(bf16), w_ref[...]))
    low = lax.broadcasted_iota(jnp.int32, (L0_SUB_ROWS, LANES), 1) < HEAD_DIM
    ones = jnp.ones((VT_ROWS - HEAD_DIM, L0_SUB_ROWS), bf16)
    for rows, proj in zip(subs, projs):
        q = proj[:, :ATTN_WIDTH]
        k = proj[:, ATTN_WIDTH:ATTN_WIDTH + KV_WIDTH]
        v = proj[:, ATTN_WIDTH + KV_WIDTH:ATTN_WIDTH + 2 * KV_WIDTH]
        g = proj[:, ATTN_WIDTH + 2 * KV_WIDTH:]
        q = _head_rms(q, e, et, qn_ref[...])
        k = _head_rms(k, e[:KV_WIDTH], et[:, :KV_WIDTH], kn_ref[...])
        if latent:
            cos, sa, sb = cos_ref[rows, :], sa_ref[rows, :], sb_ref[rows, :]
            q = _rope(q, cos, sa, sb)
            k = _rope(k, cos, sa, sb)
        else:
            knew_ref[0, rows, :] = k
            vnew_ref[0, rows, :] = v
        q_ref[0, rows, :] = q.astype(bf16)
        gs_ref[0, rows, :] = _silu(g).astype(bf16)
        v_t = v.T
        for j in range(KV_WIDTH // LANES):
            kb = k[:, LANES * j:LANES * (j + 1)]
            kbs = pltpu.roll(kb, HEAD_DIM, 1)
            kk_ref[0, 2 * j, rows, :] = jnp.where(low, kb, kbs).astype(bf16)
            kk_ref[0, 2 * j + 1, rows, :] = jnp.where(low, kbs, kb).astype(bf16)
        for kvh in range(N_KV_HEADS):
            vt_ref[0, kvh, :HEAD_DIM, rows] = v_t[HEAD_DIM * kvh:HEAD_DIM * (kvh + 1)].astype(bf16)
            vt_ref[0, kvh, HEAD_DIM:, rows] = ones


def _l0_inproj(x, mod, norm_w, w_in, qn, kn, e, et, rope_tabs, *, latent, tm):
    b, t, _ = x.shape
    attn_in = w_in.shape[1]
    row = (lambda bi, i: (bi + 1, 0, 0)) if latent else (lambda bi, i: (0, 0, 0))
    const = lambda bi, i: (0, 0)
    in_specs = [pl.BlockSpec((1, tm, D_MODEL), lambda bi, i: (bi, i, 0)),
                pl.BlockSpec((1, 1, 3 * D_MODEL), row),
                pl.BlockSpec((1, D_MODEL), const),
                pl.BlockSpec((D_MODEL, attn_in), const),
                pl.BlockSpec((1, ATTN_WIDTH), const),
                pl.BlockSpec((1, KV_WIDTH), const),
                pl.BlockSpec((ATTN_WIDTH, LANES), const),
                pl.BlockSpec((LANES, ATTN_WIDTH), const)]
    args = [x, mod, norm_w, w_in, qn, kn, e, et]
    tok = lambda bi, i: (bi, i, 0)
    out_specs = [pl.BlockSpec((1, tm, ATTN_WIDTH), tok),
                 pl.BlockSpec((1, N_KV_HEADS, tm, LANES), lambda bi, i: (bi, 0, i, 0)),
                 pl.BlockSpec((1, N_KV_HEADS, VT_ROWS, tm), lambda bi, i: (bi, 0, 0, i)),
                 pl.BlockSpec((1, tm, ATTN_WIDTH), tok)]
    out_shape = [jax.ShapeDtypeStruct((b, t, ATTN_WIDTH), bf16),
                 jax.ShapeDtypeStruct((b, N_KV_HEADS, t, LANES), bf16),
                 jax.ShapeDtypeStruct((b, N_KV_HEADS, VT_ROWS, t), bf16),
                 jax.ShapeDtypeStruct((b, t, ATTN_WIDTH), bf16)]
    if latent:
        in_specs += [pl.BlockSpec((tm, LANES), lambda bi, i: (i, 0))] * 3
        args += list(rope_tabs)
    else:
        out_specs += [pl.BlockSpec((1, tm, KV_WIDTH), tok)] * 2
        out_shape += [jax.ShapeDtypeStruct((b, t, KV_WIDTH), f32)] * 2
    return pl.pallas_call(
        functools.partial(_l0_inproj_kernel, latent=latent),
        grid=(b, t // tm),
        in_specs=in_specs, out_specs=out_specs, out_shape=out_shape,
        compiler_params=_params(2),
        name="l0_inproj_latent" if latent else "l0_inproj_context",
    )(*args)


def _attn_kernel(*refs, n_src):
    q_ref = refs[0]
    kv_refs = refs[1:1 + 2 * n_src]
    o_ref = refs[-1]
    n_sub = q_ref.shape[1] // Q_SUB_ROWS
    low = lax.broadcasted_iota(jnp.int32, (Q_SUB_ROWS, LANES), 1) < HEAD_DIM
    zero = jnp.zeros((Q_SUB_ROWS, LANES), bf16)
    chunks = []
    for i in range(n_src):
        n_keys = kv_refs[2 * i].shape[2]
        width = min(KEY_CHUNK, n_keys)
        chunks += [(kv_refs[2 * i], kv_refs[2 * i + 1], slice(c0, c0 + width))
                   for c0 in range(0, n_keys, width)]
    steps = [(sub, head, ch) for sub in range(n_sub) for head in range(4)
             for ch in range(len(chunks))]

    def q_masked(sub, head):
        qp = q_ref[0, Q_SUB_ROWS * sub:Q_SUB_ROWS * (sub + 1),
                   LANES * (head // 2):LANES * (head // 2 + 1)]
        return jnp.where(low, qp, zero) if head % 2 == 0 else jnp.where(low, zero, qp)

    def scores(step):
        sub, head, ch = step
        k_ref, _, keys = chunks[ch]
        return _dot_nt(k_ref[0, 0, keys, :], q_masked(sub, head))

    halves = []
    m = acc = None
    queue = [scores(st) for st in steps[:SCORE_LOOKAHEAD]]
    for idx, (sub, head, ch) in enumerate(steps):
        s = queue.pop(0)
        if idx + SCORE_LOOKAHEAD < len(steps):
            queue.append(scores(steps[idx + SCORE_LOOKAHEAD]))
        _, vt_ref, keys = chunks[ch]
        s = s.astype(bf16)
        slab = s.shape[0] // 4
        m_c = jnp.maximum(jnp.maximum(s[:slab], s[slab:2 * slab]),
                          jnp.maximum(s[2 * slab:3 * slab], s[3 * slab:]))
        m_c = jnp.max(m_c.astype(f32), axis=0, keepdims=True)
        m_new = m_c if ch == 0 else jnp.maximum(m, m_c)
        pv = _dot(vt_ref[0, 0, :, keys], jnp.exp2(s - m_new.astype(bf16)))
        acc = pv if ch == 0 else acc * jnp.exp2(m - m_new) + pv
        m = m_new
        if ch == len(chunks) - 1:
            halves.append(acc[:HEAD_DIM] / acc[HEAD_DIM:HEAD_DIM + 1])
            if head % 2 == 1:
                o_ref[0, Q_SUB_ROWS * sub:Q_SUB_ROWS * (sub + 1),
                      LANES * (head // 2):LANES * (head // 2 + 1)] = (
                    jnp.concatenate(halves, axis=0).T.astype(bf16))
                halves = []


def _attention(q, kv_sources, *, tq):
    b, t, _ = q.shape
    in_specs = [pl.BlockSpec((1, tq, 2 * LANES), lambda bi, g, i: (bi, i, g))]
    args = [q]
    for arr in kv_sources:
        in_specs.append(pl.BlockSpec((1, 1) + arr.shape[2:], lambda bi, g, i: (bi, g, 0, 0)))
        args.append(arr)
    return pl.pallas_call(
        functools.partial(_attn_kernel, n_src=len(kv_sources) // 2),
        grid=(b, N_KV_HEADS, t // tq),
        in_specs=in_specs,
        out_specs=pl.BlockSpec((1, tq, 2 * LANES), lambda bi, g, i: (bi, i, g)),
        out_shape=jax.ShapeDtypeStruct((b, t, ATTN_WIDTH), bf16),
        compiler_params=_params(3),
        name="attention_latent" if len(kv_sources) > 2 else "attention_context",
    )(*args)


def _mid_kernel(x_ref, xp_ref, xn_ref, o_ref, op_ref, on_ref, g_ref, gp_ref, gn_ref,
                mod0_ref, mod1_ref, wout_ref, nw_ref, wzx_ref, wdt_ref, cw_ref, cb_ref, dtb_ref,
                alog_ref, lmat_ref, umat_ref,
                x1_ref, z_ref, xs_ref, bc_ref, acs_ref, rowt_ref, w_ref, lsum_ref):
    i = pl.program_id(1)
    last = pl.num_programs(1) - 1
    tm = x_ref.shape[1]
    rows = tm + 2 * HALO
    body = slice(HALO, HALO + tm)
    cat = lambda a, b_, c: jnp.concatenate([a[0], b_[0], c[0]], axis=0)
    gated = cat(op_ref, o_ref, on_ref) * cat(gp_ref, g_ref, gn_ref)
    x1 = cat(xp_ref, x_ref, xn_ref) + mod0_ref[0][:, 2 * D_MODEL:] * _dot(gated, wout_ref[...])
    x1_ref[0] = x1[body]
    mod = mod1_ref[0]
    h = _modulated_norm(x1, nw_ref[...], mod[:, :D_MODEL], mod[:, D_MODEL:2 * D_MODEL])
    h_body = h[body].astype(bf16)
    h = jnp.concatenate([jnp.where(i == 0, 0.0, h[:HALO]).astype(bf16), h_body,
                         jnp.where(i == last, 0.0, h[HALO + tm:]).astype(bf16)], axis=0)

    n_z = D_INNER // PROJ_BLOCK
    tasks = []
    for j in range(CONV_DIM // PROJ_BLOCK):
        tasks.append(("conv", j))
        if j < n_z:
            tasks.append(("z", j))

    def project(task):
        kind, j = task
        if kind == "z":
            return _dot(h_body, wzx_ref[:, PROJ_BLOCK * j:PROJ_BLOCK * (j + 1)])
        return _dot(h, wzx_ref[:, D_INNER + PROJ_BLOCK * j:D_INNER + PROJ_BLOCK * (j + 1)])

    queue = [project(t) for t in tasks[:PROJ_LOOKAHEAD]]
    for idx, (kind, j) in enumerate(tasks):
        blk = queue.pop(0)
        if idx + PROJ_LOOKAHEAD < len(tasks):
            queue.append(project(tasks[idx + PROJ_LOOKAHEAD]))
        cols = slice(PROJ_BLOCK * j, PROJ_BLOCK * (j + 1))
        if kind == "z":
            z_ref[0, :, cols] = blk.astype(bf16)
            continue
        act = _silu(cw_ref[0:1, cols] * pltpu.roll(blk, 1, 0)[body] + cw_ref[1:2, cols] * blk[body]
                    + cw_ref[2:3, cols] * pltpu.roll(blk, rows - 1, 0)[body]
                    + cb_ref[:, cols]).astype(bf16)
        if j < n_z:
            xs_ref[0, :, cols] = act
        else:
            bc_ref[0, :, PROJ_BLOCK * (j - n_z):PROJ_BLOCK * (j - n_z + 1)] = act

    dt = _softplus(_dot(h_body, wdt_ref[...]) + dtb_ref[...])
    a = -jnp.exp(alog_ref[...])
    fwd_lane = lax.broadcasted_iota(jnp.int32, (CHUNK, LANES), 1) < SSD_HEADS
    lower, upper = lmat_ref[...], umat_ref[...]
    for c in range(tm // CHUNK):
        dtc = dt[CHUNK * c:CHUNK * (c + 1)]
        ah, al = _split(dtc * a)
        acs = jnp.where(fwd_lane, _dot(lower, ah) + _dot(lower, al), _dot(upper, ah) + _dot(upper, al))
        end_row = jnp.where(fwd_lane[0:1], acs[CHUNK - 1:CHUNK], acs[0:1])
        acs_ref[0, CHUNK * c:CHUNK * (c + 1)] = acs
        w_ref[0, CHUNK * c:CHUNK * (c + 1)] = dtc * jnp.exp(end_row - acs)
        rowt_ref[0, c] = (acs - jnp.log(dtc)).T
        lsum_ref[0, c] = jnp.log(dtc + pltpu.roll(dtc, LANES - SSD_HEADS, 1)).T


def _mid(x, o, gs, mod0, mod1, w_out0, norm_w, w_zx, w_dt, conv_w, conv_b, dt_bias, alog, lmat, umat,
         *, latent, tm):
    b, t, _ = x.shape
    row = (lambda bi, i: (bi + 1, 0, 0)) if latent else (lambda bi, i: (0, 0, 0))
    const = lambda bi, i: (0, 0)
    tok = lambda bi, i: (bi, i, 0)
    hb = tm // HALO
    n_hb = t // HALO
    prev_map = lambda bi, i: (bi, jnp.maximum(i * hb - 1, 0), 0)
    next_map = lambda bi, i: (bi, jnp.minimum((i + 1) * hb, n_hb - 1), 0)
    halo3 = lambda width: [pl.BlockSpec((1, tm, width), tok),
                           pl.BlockSpec((1, HALO, width), prev_map),
                           pl.BlockSpec((1, HALO, width), next_map)]
    resident = lambda shape: pl.BlockSpec(shape, const, pipeline_mode=pl.Buffered(1))
    chunk4 = lambda bi, i: (bi, i, 0, 0)
    ncs = tm // CHUNK
    return pl.pallas_call(
        _mid_kernel,
        grid=(b, t // tm),
        in_specs=halo3(D_MODEL) + halo3(ATTN_WIDTH) + halo3(ATTN_WIDTH) + [
            pl.BlockSpec((1, 1, 3 * D_MODEL), row),
            pl.BlockSpec((1, 1, 3 * D_MODEL), row),
            resident((ATTN_WIDTH, D_MODEL)),
            pl.BlockSpec((1, D_MODEL), const),
            resident((D_MODEL, D_INNER + CONV_DIM)),
            resident((D_MODEL, LANES)),
            pl.BlockSpec((3, CONV_DIM), const),
            pl.BlockSpec((1, CONV_DIM), const),
            pl.BlockSpec((1, LANES), const),
            pl.BlockSpec((1, LANES), const),
            pl.BlockSpec((CHUNK, CHUNK), const),
            pl.BlockSpec((CHUNK, CHUNK), const)],
        out_specs=[pl.BlockSpec((1, tm, D_MODEL), tok),
                   pl.BlockSpec((1, tm, D_INNER), tok),
                   pl.BlockSpec((1, tm, D_INNER), tok),
                   pl.BlockSpec((1, tm, BC_WIDTH), tok),
                   pl.BlockSpec((1, tm, LANES), tok),
                   pl.BlockSpec((1, ncs, LANES, CHUNK), chunk4),
                   pl.BlockSpec((1, tm, LANES), tok),
                   pl.BlockSpec((1, ncs, LANES, CHUNK), chunk4)],
        out_shape=[jax.ShapeDtypeStruct((b, t, D_MODEL), f32),
                   jax.ShapeDtypeStruct((b, t, D_INNER), bf16),
                   jax.ShapeDtypeStruct((b, t, D_INNER), bf16),
                   jax.ShapeDtypeStruct((b, t, BC_WIDTH), bf16),
                   jax.ShapeDtypeStruct((b, t, LANES), f32),
                   jax.ShapeDtypeStruct((b, t // CHUNK, LANES, CHUNK), f32),
                   jax.ShapeDtypeStruct((b, t, LANES), f32),
                   jax.ShapeDtypeStruct((b, t // CHUNK, LANES, CHUNK), f32)],
        compiler_params=_params(2),
        name="mid_latent" if latent else "mid_context",
    )(x, x, x, o, o, o, gs, gs, gs, mod0, mod1, w_out0, norm_w, w_zx, w_dt, conv_w, conv_b,
      dt_bias, alog, lmat, umat)


def _init_state(ht_ref, h0_ref):
    if h0_ref is None:
        ht_ref[...] = jnp.zeros_like(ht_ref)
    else:
        for j in range(SSD_HEADS // 2):
            ht_ref[:, LANES * j:LANES * (j + 1)] = h0_ref[0, LANES * j:LANES * (j + 1), :].T


def _write_state(hfin_ref, ht_ref):
    for j in range(SSD_HEADS // 2):
        hfin_ref[0, LANES * j:LANES * (j + 1), :] = ht_ref[:, LANES * j:LANES * (j + 1)].T


def _lane_expand_decay(acs_row, e_ref):
    hi, lo = _split(jnp.broadcast_to(jnp.exp(acs_row), (HALF_VREG_ROWS, LANES)))
    return (_dot(hi, e_ref[...]) + _dot(lo, e_ref[...]))[0:1]


def _weighted_x(xs_ref, w_ref, e_ref, rows, g):
    gcols = slice(GROUP_W * g, GROUP_W * (g + 1))
    return xs_ref[0, rows, gcols] * _dot(w_ref[0, rows, :].astype(bf16), e_ref[:, gcols]).astype(bf16)


def _update_state(ht_ref, bc_ref, rows, g, xw, chunk_decay):
    gcols = slice(GROUP_W * g, GROUP_W * (g + 1))
    bm = bc_ref[0, rows, D_STATE * g:D_STATE * (g + 1)]
    st = _dot(bm.astype(f32).T.astype(bf16), xw)
    ht_ref[:, gcols] = ht_ref[:, gcols] * chunk_decay[:, gcols] + st


def _ssd_states_kernel(*refs, has_h0, final_state):
    xs_ref, bc_ref, acs_ref, w_ref, e_ref = refs[:5]
    refs = refs[5:]
    h0_ref = None
    if has_h0:
        h0_ref, refs = refs[0], refs[1:]
    hprev_ref, refs = refs[0], refs[1:]
    if final_state:
        hfin_ref, refs = refs[0], refs[1:]
    ht_ref = refs[0]
    step = pl.program_id(1)

    @pl.when(step == 0)
    def _init():
        _init_state(ht_ref, h0_ref)

    n_chunks = xs_ref.shape[1] // CHUNK
    xws = {(k, g): _weighted_x(xs_ref, w_ref, e_ref, slice(CHUNK * k, CHUNK * (k + 1)), g)
           for k in range(n_chunks) for g in range(SSD_GROUPS)}
    for k in range(n_chunks):
        rows = slice(CHUNK * k, CHUNK * (k + 1))
        chunk_decay = _lane_expand_decay(acs_ref[0, CHUNK * (k + 1) - 1:CHUNK * (k + 1), :], e_ref)
        hprev_ref[0, k] = ht_ref[...].astype(bf16)
        for g in range(SSD_GROUPS):
            _update_state(ht_ref, bc_ref, rows, g, xws[k, g], chunk_decay)

    if final_state:
        @pl.when(step == pl.num_programs(1) - 1)
        def _fin():
            _write_state(hfin_ref, ht_ref)


def _ssd_main_kernel(*refs, has_h0, final_state):
    xs_ref, bc_ref, acs_ref, rowt_ref, lsum_ref, w_ref, e_ref, hprev_ref = refs[:8]
    refs = refs[8:]
    h0_ref = None
    if has_h0:
        h0_ref, refs = refs[0], refs[1:]
    z_ref, x1_ref, mod_ref, gw_ref, wout_ref, dskip_ref = refs[:6]
    out_ref, refs = refs[6], refs[7:]
    if final_state:
        hfin_ref, refs = refs[0], refs[1:]
    ht_ref, y_ref, yn_ref = refs
    step = pl.program_id(1)

    @pl.when(step == 0)
    def _init():
        _init_state(ht_ref, h0_ref)

    pairs_per_group = HEADS_PER_GROUP // 2
    n_chunks = xs_ref.shape[1] // CHUNK
    ri = lax.broadcasted_iota(jnp.int32, (CHUNK, CHUNK), 0)
    ci = lax.broadcasted_iota(jnp.int32, (CHUNK, CHUNK), 1)
    past, future = ri > ci, ri < ci
    low = lax.broadcasted_iota(jnp.int32, (CHUNK, LANES), 1) < SSD_HEAD_DIM
    chunk_order = list(range(n_chunks - 1, -1, -1))

    cbs, xws = {}, {}
    for k in chunk_order:
        rows = slice(CHUNK * k, CHUNK * (k + 1))
        for g in range(SSD_GROUPS):
            bm = bc_ref[0, rows, D_STATE * g:D_STATE * (g + 1)]
            cm = bc_ref[0, rows, D_STATE * (SSD_GROUPS + g):D_STATE * (SSD_GROUPS + g + 1)]
            cbs[k, g] = _dot_nt(cm, bm)
            xws[k, g] = _weighted_x(xs_ref, w_ref, e_ref, rows, g)

    for k in chunk_order:
        rows = slice(CHUNK * k, CHUNK * (k + 1))
        acs = acs_ref[0, rows, :]
        rowt = rowt_ref[0, k]
        lsum = lsum_ref[0, k]
        chunk_decay = _lane_expand_decay(acs[0:1, :], e_ref)
        for g in range(SSD_GROUPS):
            cm_f = bc_ref[0, rows, D_STATE * (SSD_GROUPS + g):D_STATE * (SSD_GROUPS + g + 1)].astype(f32)
            cb = cbs[k, g]
            for jp in range(pairs_per_group):
                j = g * pairs_per_group + jp
                cols = slice(LANES * j, LANES * (j + 1))
                rhs = jnp.concatenate([xs_ref[0, rows, cols], hprev_ref[0, k, :, cols],
                                       ht_ref[:, cols].astype(bf16)], axis=0)
                ys = []
                for par in range(2):
                    hf = 2 * j + par
                    hb = SSD_HEADS + hf
                    col_f = jnp.broadcast_to(acs[:, hf:hf + 1], (CHUNK, CHUNK))
                    col_b = jnp.broadcast_to(acs[:, hb:hb + 1], (CHUNK, CHUNK))
                    expo = jnp.where(past, col_f - rowt[hf:hf + 1, :],
                                     jnp.where(future, col_b - rowt[hb:hb + 1, :], lsum[hf:hf + 1, :]))
                    lhs = jnp.concatenate([(cb * jnp.exp(expo)).astype(bf16),
                                           (cm_f * jnp.exp(col_f)).astype(bf16),
                                           (cm_f * jnp.exp(col_b)).astype(bf16)], axis=1)
                    ys.append(_dot(lhs, rhs))
                y_ref[:, cols] = jnp.where(low, ys[0], ys[1])
            _update_state(ht_ref, bc_ref, rows, g, xws[k, g], chunk_decay)

        y = y_ref[...] + dskip_ref[...] * xs_ref[0, rows, :].astype(f32)
        yz = y * _silu(z_ref[0, rows, :].astype(f32))
        ms = jnp.mean(yz * yz, axis=-1, keepdims=True)
        yn_ref[rows, :] = (yz * lax.rsqrt(ms + EPS) * gw_ref[...]).astype(bf16)

    gate = mod_ref[0][:, 2 * D_MODEL:]
    out_ref[0] = x1_ref[0] + gate * _dot(yn_ref[...], wout_ref[...])

    if final_state:
        @pl.when(step == pl.num_programs(1) - 1)
        def _fin():
            _write_state(hfin_ref, ht_ref)


def _ssd_states(xs, bc, acs, w, e_fwd, h0, *, latent):
    b, t, _ = xs.shape
    rows = SSD_STEP_CHUNKS * CHUNK
    blk = lambda bi, c: (bi, c, 0)
    per_b = lambda bi, c: (bi, 0, 0)
    in_specs = [pl.BlockSpec((1, rows, D_INNER), blk),
                pl.BlockSpec((1, rows, BC_WIDTH), blk),
                pl.BlockSpec((1, rows, LANES), blk),
                pl.BlockSpec((1, rows, LANES), blk),
                pl.BlockSpec((LANES, D_INNER), lambda bi, c: (0, 0))]
    args = [xs, bc, acs, w, e_fwd]
    if h0 is not None:
        in_specs.append(pl.BlockSpec((1, D_INNER, D_STATE), per_b))
        args.append(h0)
    out_specs = [pl.BlockSpec((1, SSD_STEP_CHUNKS, D_STATE, D_INNER), lambda bi, c: (bi, c, 0, 0))]
    out_shape = [jax.ShapeDtypeStruct((b, t // CHUNK, D_STATE, D_INNER), bf16)]
    if not latent:
        out_specs.append(pl.BlockSpec((1, D_INNER, D_STATE), per_b))
        out_shape.append(jax.ShapeDtypeStruct((b, D_INNER, D_STATE), f32))
    return pl.pallas_call(
        functools.partial(_ssd_states_kernel, has_h0=h0 is not None, final_state=not latent),
        grid=(b, t // rows),
        in_specs=in_specs, out_specs=out_specs, out_shape=out_shape,
        scratch_shapes=[pltpu.VMEM((D_STATE, D_INNER), f32)],
        compiler_params=_params(2),
        name="ssd_states",
    )(*args)


def _ssd_main(xs, bc, acs, rowt, lsum, w, e_bwd, hprev, h0, z, x1, mod, gw, wout, dskip, *, latent):
    b, t, _ = xs.shape
    rows = SSD_STEP_CHUNKS * CHUNK
    n_steps = t // rows
    blk = lambda bi, c: (bi, n_steps - 1 - c, 0)
    blk4 = lambda bi, c: (bi, n_steps - 1 - c, 0, 0)
    const = lambda bi, c: (0, 0)
    per_b = lambda bi, c: (bi, 0, 0)
    row = (lambda bi, c: (bi + 1, 0, 0)) if latent else (lambda bi, c: (0, 0, 0))
    in_specs = [pl.BlockSpec((1, rows, D_INNER), blk),
                pl.BlockSpec((1, rows, BC_WIDTH), blk),
                pl.BlockSpec((1, rows, LANES), blk),
                pl.BlockSpec((1, SSD_STEP_CHUNKS, LANES, CHUNK), blk4),
                pl.BlockSpec((1, SSD_STEP_CHUNKS, LANES, CHUNK), blk4),
                pl.BlockSpec((1, rows, LANES), blk),
                pl.BlockSpec((LANES, D_INNER), const),
                pl.BlockSpec((1, SSD_STEP_CHUNKS, D_STATE, D_INNER), blk4)]
    args = [xs, bc, acs, rowt, lsum, w, e_bwd, hprev]
    if h0 is not None:
        in_specs.append(pl.BlockSpec((1, D_INNER, D_STATE), per_b))
        args.append(h0)
    in_specs += [pl.BlockSpec((1, rows, D_INNER), blk),
                 pl.BlockSpec((1, rows, D_MODEL), blk),
                 pl.BlockSpec((1, 1, 3 * D_MODEL), row),
                 pl.BlockSpec((1, D_INNER), const),
                 pl.BlockSpec((D_INNER, D_MODEL), const),
                 pl.BlockSpec((1, D_INNER), const)]
    args += [z, x1, mod, gw, wout, dskip]
    out_specs = [pl.BlockSpec((1, rows, D_MODEL), blk)]
    out_shape = [jax.ShapeDtypeStruct((b, t, D_MODEL), f32)]
    if not latent:
        out_specs.append(pl.BlockSpec((1, D_INNER, D_STATE), per_b))
        out_shape.append(jax.ShapeDtypeStruct((b, D_INNER, D_STATE), f32))
    return pl.pallas_call(
        functools.partial(_ssd_main_kernel, has_h0=h0 is not None, final_state=not latent),
        grid=(b, n_steps),
        in_specs=in_specs, out_specs=out_specs, out_shape=out_shape,
        scratch_shapes=[pltpu.VMEM((D_STATE, D_INNER), f32), pltpu.VMEM((CHUNK, D_INNER), f32),
                        pltpu.VMEM((rows, D_INNER), bf16)],
        compiler_params=_params(2),
        name="ssd_main",
    )(*args)


def _rope_tables(n_tokens):
    rows = n_tokens // GRID_W
    row_ids = jnp.repeat(jnp.arange(rows), GRID_W).astype(f32)
    col_ids = jnp.tile(jnp.arange(GRID_W), rows).astype(f32)
    inv_freq = 1.0 / (ROPE_THETA ** (jnp.arange(0, AXIS_DIM, 2, dtype=f32) / AXIS_DIM))
    ang = jnp.stack([row_ids[:, None] * inv_freq, col_ids[:, None] * inv_freq], axis=1)
    cos, sin, zero = jnp.cos(ang), jnp.sin(ang), jnp.zeros_like(ang)
    head = lambda a, b_: jnp.stack([a, b_], axis=2).reshape(n_tokens, HEAD_DIM)
    two = lambda u: jnp.concatenate([u, u], axis=1)
    return two(head(cos, cos)), two(head(-sin, zero)), two(head(zero, sin))


def kernel(x_prompt, x_sample, cache_k_l0, cache_v_l0, state_fwd_l1, state_bwd_l1, c, c_ctx,
           l0_norm_w, l0_mod_w, l0_mod_b, l0_w_in, l0_q_norm, l0_k_norm, l0_w_out,
           l1_norm_w, l1_mod_w, l1_mod_b, l1_w_in, l1_conv_w, l1_conv_b, l1_dt_bias_f, l1_dt_bias_b,
           l1_a_log_f, l1_a_log_b, l1_d_skip, l1_gnorm_w, l1_w_out):
    dec_b = x_sample.shape[0]
    pad = lambda u, n: jnp.pad(u, (0, n - u.shape[0]))

    n_rows = 16
    cond = jnp.zeros((n_rows, D_MODEL), f32).at[0].set(c_ctx).at[1:1 + dec_b].set(c)
    mod0 = _ada_mod(cond, l0_mod_w, l0_mod_b).reshape(n_rows, 1, 3 * D_MODEL)
    mod1 = _ada_mod(cond, l1_mod_w, l1_mod_b).reshape(n_rows, 1, 3 * D_MODEL)

    q_scale = HEAD_DIM ** -0.5 * math.log2(math.e)
    qn = (jnp.tile(l0_q_norm, N_HEADS) * q_scale).reshape(1, ATTN_WIDTH)
    kn = jnp.tile(l0_k_norm, N_KV_HEADS).reshape(1, KV_WIDTH)
    head_of = jnp.arange(ATTN_WIDTH) // HEAD_DIM
    e = (head_of[:, None] == jnp.arange(LANES)[None, :]).astype(bf16)
    et = e.T
    w_in0 = l0_w_in.astype(bf16)
    w_out0 = l0_w_out.astype(bf16)
    w_zx = l1_w_in[:, :D_INNER + CONV_DIM].astype(bf16)
    w_dt = jnp.pad(l1_w_in[:, D_INNER + CONV_DIM:], ((0, 0), (0, LANES - 2 * SSD_HEADS))).astype(bf16)
    w_out1 = l1_w_out.astype(bf16)
    dt_bias = pad(jnp.concatenate([l1_dt_bias_f, l1_dt_bias_b]), LANES).reshape(1, LANES)
    alog = pad(jnp.concatenate([l1_a_log_f, l1_a_log_b]), LANES).reshape(1, LANES)
    dskip = jnp.repeat(l1_d_skip, SSD_HEAD_DIM).reshape(1, D_INNER)
    nw0 = l0_norm_w.reshape(1, D_MODEL)
    nw1 = l1_norm_w.reshape(1, D_MODEL)
    gw = l1_gnorm_w.reshape(1, D_INNER)
    conv_b = l1_conv_b.reshape(1, CONV_DIM)
    tri = jnp.arange(CHUNK)
    lmat = (tri[None, :] <= tri[:, None]).astype(bf16)
    umat = lmat.T
    rope_tabs = _rope_tables(x_sample.shape[1])
    ssd_head_of = jnp.arange(D_INNER) // SSD_HEAD_DIM
    e_fwd = (jnp.arange(LANES)[:, None] == ssd_head_of[None, :]).astype(bf16)
    e_bwd = (jnp.arange(LANES)[:, None] == SSD_HEADS + ssd_head_of[None, :]).astype(bf16)

    kc = cache_k_l0.transpose(0, 2, 1, 3).astype(bf16)
    vc = cache_v_l0.transpose(0, 2, 3, 1).astype(bf16)
    kk_ctx = jnp.concatenate([kc, kc], axis=-1)
    vt_ctx = jnp.concatenate(
        [vc, jnp.ones(vc.shape[:2] + (VT_ROWS - HEAD_DIM, vc.shape[3]), bf16)], axis=2)
    h0_f = state_fwd_l1.reshape(dec_b, D_INNER, D_STATE)
    h0_b = state_bwd_l1.reshape(dec_b, D_INNER, D_STATE)

    outs = {}
    for latent, x in ((False, x_prompt), (True, x_sample)):
        res = _l0_inproj(x, mod0, nw0, w_in0, qn, kn, e, et, rope_tabs, latent=latent,
                         tm=2 * L0_SUB_ROWS if latent else L0_SUB_ROWS)
        if latent:
            q, kk, vt, gs = res
            kv = [kk_ctx, vt_ctx, kk, vt]
        else:
            q, kk, vt, gs, k_new, v_new = res
            kv = [kk, vt]
        o = _attention(q, kv, tq=4 * Q_SUB_ROWS if latent else Q_SUB_ROWS)
        x1, z, xs, bc, acs, rowt, w, lsum = _mid(
            x, o, gs, mod0, mod1, w_out0, nw1, w_zx, w_dt, l1_conv_w, conv_b, dt_bias, alog,
            lmat, umat, latent=latent, tm=512 if latent else 256)
        fwd = _ssd_states(xs, bc, acs, w, e_fwd, h0_f if latent else None, latent=latent)
        bwd = _ssd_main(xs, bc, acs, rowt, lsum, w, e_bwd, fwd[0], h0_b if latent else None,
                        z, x1, mod1, gw, w_out1, dskip, latent=latent)
        if latent:
            outs["y_sample"] = bwd[0]
        else:
            b = x.shape[0]
            outs["y_prompt"] = bwd[0]
            outs["k"] = k_new.reshape(b, -1, N_KV_HEADS, HEAD_DIM)
            outs["v"] = v_new.reshape(b, -1, N_KV_HEADS, HEAD_DIM)
            outs["hf"] = fwd[1].reshape(b, SSD_HEADS, SSD_HEAD_DIM, D_STATE)
            outs["hb"] = bwd[1].reshape(b, SSD_HEADS, SSD_HEAD_DIM, D_STATE)
    return (outs["y_prompt"], outs["y_sample"], outs["k"], outs["v"], outs["hf"], outs["hb"])
```

```python
import functools
import math

import jax
import jax.numpy as jnp
from jax import lax
from jax.experimental import pallas as pl
from jax.experimental.pallas import tpu as pltpu

f32 = jnp.float32
bf16 = jnp.bfloat16

D_MODEL = 1024
GRID_W = 64
EPS = 1e-6
N_HEADS = 16
N_KV_HEADS = 4
HEAD_DIM = 64
ATTN_WIDTH = N_HEADS * HEAD_DIM
KV_WIDTH = N_KV_HEADS * HEAD_DIM
AXIS_DIM = HEAD_DIM // 2
ROPE_THETA = 10000.0
D_INNER = 2048
SSD_HEAD_DIM = 64
SSD_HEADS = D_INNER // SSD_HEAD_DIM
SSD_GROUPS = 4
HEADS_PER_GROUP = SSD_HEADS // SSD_GROUPS
D_STATE = 128
CHUNK = 128
BC_WIDTH = 2 * SSD_GROUPS * D_STATE
CONV_DIM = D_INNER + BC_WIDTH

LANES = 128
BF16_ROWS = 16
HALO = BF16_ROWS
PROJ_BLOCK = 256
PROJ_LOOKAHEAD = 2
L0_SUB_ROWS = 256
SSD_STEP_CHUNKS = 4
HALF_VREG_ROWS = 8
GROUP_W = HEADS_PER_GROUP * SSD_HEAD_DIM
VT_ROWS = HEAD_DIM + BF16_ROWS
Q_SUB_ROWS = 256
KEY_CHUNK = 256
SCORE_LOOKAHEAD = 8
VMEM_LIMIT = 56 * 1024 * 1024

NEG_BIG = -1e30


def _params(n_axes):
    return pltpu.CompilerParams(
        dimension_semantics=("arbitrary",) * n_axes, vmem_limit_bytes=VMEM_LIMIT)


def _dot(a, b):
    return jnp.dot(a, b, preferred_element_type=f32)


def _dot_nt(a, b):
    return lax.dot_general(a, b, (((1,), (1,)), ((), ())), preferred_element_type=f32)


def _split(x):
    hi = x.astype(bf16)
    lo = (x - hi.astype(f32)).astype(bf16)
    return hi, lo


def _silu(x):
    return x * jax.nn.sigmoid(x)


def _softplus(x):
    return jnp.maximum(x, 0.0) + jnp.log1p(jnp.exp(-jnp.abs(x)))


def _modulated_norm(x, norm_w, shift, scale):
    ms = jnp.mean(x * x, axis=-1, keepdims=True)
    return (x * lax.rsqrt(ms + EPS) * norm_w) * (1.0 + scale) + shift


def _ada_mod_kernel(cond_ref, w_ref, b_ref, o_ref):
    s = _silu(cond_ref[...])
    sh, sl = _split(s)
    wh, wl = _split(w_ref[...])
    o_ref[...] = _dot(sh, wh) + (_dot(sh, wl) + _dot(sl, wh)) + b_ref[...]


def _ada_mod(cond, mod_w, mod_b):
    rows = cond.shape[0]
    tn = 1024
    return pl.pallas_call(
        _ada_mod_kernel,
        grid=(3 * D_MODEL // tn,),
        in_specs=[pl.BlockSpec((rows, D_MODEL), lambda j: (0, 0)),
                  pl.BlockSpec((D_MODEL, tn), lambda j: (0, j)),
                  pl.BlockSpec((1, tn), lambda j: (0, j))],
        out_specs=pl.BlockSpec((rows, tn), lambda j: (0, j)),
        out_shape=jax.ShapeDtypeStruct((rows, 3 * D_MODEL), f32),
        compiler_params=_params(1),
        name="ada_mod",
    )(cond, mod_w, mod_b.reshape(1, -1))


def _head_rms(t, e, et, w):
    hi, lo = _split(t * t)
    ss = _dot(hi, e) + _dot(lo, e)
    rh, rl = _split(lax.rsqrt(ss * (1.0 / HEAD_DIM) + EPS))
    return t * (_dot(rh, et) + _dot(rl, et)) * w


def _rope(t, cos, sin_a, sin_b):
    width = t.shape[1]
    rep = width // LANES
    tile = lambda u: jnp.concatenate([u] * rep, axis=1)
    return (t * tile(cos) + pltpu.roll(t, width - AXIS_DIM // 2, 1) * tile(sin_a)
            + pltpu.roll(t, AXIS_DIM // 2, 1) * tile(sin_b))


def _l0_inproj_kernel(*refs, latent):
    x_ref, mod_ref, nw_ref, w_ref, qn_ref, kn_ref, e_ref, et_ref = refs[:8]
    refs = refs[8:]
    if latent:
        cos_ref, sa_ref, sb_ref = refs[:3]
        q_ref, kk_ref, vt_ref, gs_ref = refs[3:]
    else:
        q_ref, kk_ref, vt_ref, gs_ref, knew_ref, vnew_ref = refs
    mod = mod_ref[0]
    e = e_ref[...]
    et = et_ref[...]
    subs = [slice(r0, r0 + L0_SUB_ROWS) for r0 in range(0, x_ref.shape[1], L0_SUB_ROWS)]
    projs = []
    for rows in subs:
        h = _modulated_norm(x_ref[0, rows, :], nw_ref[...], mod[:, :D_MODEL], mod[:, D_MODEL:2 * D_MODEL])
        projs.append(_dot(h.astype(bf16), w_ref[...]))
    low = lax.broadcasted_iota(jnp.int32, (L0_SUB_ROWS, LANES), 1) < HEAD_DIM
    ones = jnp.ones((VT_ROWS - HEAD_DIM, L0_SUB_ROWS), bf16)
    for rows, proj in zip(subs, projs):
        q = proj[:, :ATTN_WIDTH]
        k = proj[:, ATTN_WIDTH:ATTN_WIDTH + KV_WIDTH]
        v = proj[:, ATTN_WIDTH + KV_WIDTH:ATTN_WIDTH + 2 * KV_WIDTH]
        g = proj[:, ATTN_WIDTH + 2 * KV_WIDTH:]
        q = _head_rms(q, e, et, qn_ref[...])
        k = _head_rms(k, e[:KV_WIDTH], et[:, :KV_WIDTH], kn_ref[...])
        if latent:
            cos, sa, sb = cos_ref[rows, :], sa_ref[rows, :], sb_ref[rows, :]
            q = _rope(q, cos, sa, sb)
            k = _rope(k, cos, sa, sb)
        else:
            knew_ref[0, rows, :] = k
            vnew_ref[0, rows, :] = v
        q_ref[0, rows, :] = q.astype(bf16)
        gs_ref[0, rows, :] = _silu(g).astype(bf16)
        v_t = v.T
        for j in range(KV_WIDTH // LANES):
            kb = k[:, LANES * j:LANES * (j + 1)]
            kbs = pltpu.roll(kb, HEAD_DIM, 1)
            kk_ref[0, 2 * j, rows, :] = jnp.where(low, kb, kbs).astype(bf16)
            kk_ref[0, 2 * j + 1, rows, :] = jnp.where(low, kbs, kb).astype(bf16)
        for kvh in range(N_KV_HEADS):
            vt_ref[0, kvh, :HEAD_DIM, rows] = v_t[HEAD_DIM * kvh:HEAD_DIM * (kvh + 1)].astype(bf16)
            vt_ref[0, kvh, HEAD_DIM:, rows] = ones


def _l0_inproj(x, mod, norm_w, w_in, qn, kn, e, et, rope_tabs, *, latent, tm):
    b, t, _ = x.shape
    attn_in = w_in.shape[1]
    row = (lambda bi, i: (bi + 1, 0, 0)) if latent else (lambda bi, i: (0, 0, 0))
    const = lambda bi, i: (0, 0)
    in_specs = [pl.BlockSpec((1, tm, D_MODEL), lambda bi, i: (bi, i, 0)),
                pl.BlockSpec((1, 1, 3 * D_MODEL), row),
                pl.BlockSpec((1, D_MODEL), const),
                pl.BlockSpec((D_MODEL, attn_in), const),
                pl.BlockSpec((1, ATTN_WIDTH), const),
                pl.BlockSpec((1, KV_WIDTH), const),
                pl.BlockSpec((ATTN_WIDTH, LANES), const),
                pl.BlockSpec((LANES, ATTN_WIDTH), const)]
    args = [x, mod, norm_w, w_in, qn, kn, e, et]
    tok = lambda bi, i: (bi, i, 0)
    out_specs = [pl.BlockSpec((1, tm, ATTN_WIDTH), tok),
                 pl.BlockSpec((1, N_KV_HEADS, tm, LANES), lambda bi, i: (bi, 0, i, 0)),
                 pl.BlockSpec((1, N_KV_HEADS, VT_ROWS, tm), lambda bi, i: (bi, 0, 0, i)),
                 pl.BlockSpec((1, tm, ATTN_WIDTH), tok)]
    out_shape = [jax.ShapeDtypeStruct((b, t, ATTN_WIDTH), bf16),
                 jax.ShapeDtypeStruct((b, N_KV_HEADS, t, LANES), bf16),
                 jax.ShapeDtypeStruct((b, N_KV_HEADS, VT_ROWS, t), bf16),
                 jax.ShapeDtypeStruct((b, t, ATTN_WIDTH), bf16)]
    if latent:
        in_specs += [pl.BlockSpec((tm, LANES), lambda bi, i: (i, 0))] * 3
        args += list(rope_tabs)
    else:
        out_specs += [pl.BlockSpec((1, tm, KV_WIDTH), tok)] * 2
        out_shape += [jax.ShapeDtypeStruct((b, t, KV_WIDTH), f32)] * 2
    return pl.pallas_call(
        functools.partial(_l0_inproj_kernel, latent=latent),
        grid=(b, t // tm),
        in_specs=in_specs, out_specs=out_specs, out_shape=out_shape,
        compiler_params=_params(2),
        name="l0_inproj_latent" if latent else "l0_inproj_context",
    )(*args)


def _attn_kernel(*refs, n_src):
    q_ref = refs[0]
    kv_refs = refs[1:1 + 2 * n_src]
    o_ref = refs[-1]
    n_sub = q_ref.shape[1] // Q_SUB_ROWS
    low = lax.broadcasted_iota(jnp.int32, (Q_SUB_ROWS, LANES), 1) < HEAD_DIM
    zero = jnp.zeros((Q_SUB_ROWS, LANES), bf16)
    chunks = []
    for i in range(n_src):
        n_keys = kv_refs[2 * i].shape[2]
        width = min(KEY_CHUNK, n_keys)
        chunks += [(kv_refs[2 * i], kv_refs[2 * i + 1], slice(c0, c0 + width))
                   for c0 in range(0, n_keys, width)]
    steps = [(sub, head, ch) for sub in range(n_sub) for head in range(4)
             for ch in range(len(chunks))]

    def q_masked(sub, head):
        qp = q_ref[0, Q_SUB_ROWS * sub:Q_SUB_ROWS * (sub + 1),
                   LANES * (head // 2):LANES * (head // 2 + 1)]
        return jnp.where(low, qp, zero) if head % 2 == 0 else jnp.where(low, zero, qp)

    def scores(step):
        sub, head, ch = step
        k_ref, _, keys = chunks[ch]
        return _dot_nt(k_ref[0, 0, keys, :], q_masked(sub, head))

    halves = []
    m = acc = None
    queue = [scores(st) for st in steps[:SCORE_LOOKAHEAD]]
    for idx, (sub, head, ch) in enumerate(steps):
        s = queue.pop(0)
        if idx + SCORE_LOOKAHEAD < len(steps):
            queue.append(scores(steps[idx + SCORE_LOOKAHEAD]))
        _, vt_ref, keys = chunks[ch]
        s = s.astype(bf16)
        slab = s.shape[0] // 4
        m_c = jnp.maximum(jnp.maximum(s[:slab], s[slab:2 * slab]),
                          jnp.maximum(s[2 * slab:3 * slab], s[3 * slab:]))
        m_c = jnp.max(m_c.astype(f32), axis=0, keepdims=True)
        m_new = m_c if ch == 0 else jnp.maximum(m, m_c)
        pv = _dot(vt_ref[0, 0, :, keys], jnp.exp2(s - m_new.astype(bf16)))
        acc = pv if ch == 0 else acc * jnp.exp2(m - m_new) + pv
        m = m_new
        if ch == len(chunks) - 1:
            halves.append(acc[:HEAD_DIM] / acc[HEAD_DIM:HEAD_DIM + 1])
            if head % 2 == 1:
                o_ref[0, Q_SUB_ROWS * sub:Q_SUB_ROWS * (sub + 1),
                      LANES * (head // 2):LANES * (head // 2 + 1)] = (
                    jnp.concatenate(halves, axis=0).T.astype(bf16))
                halves = []


def _attention(q, kv_sources, *, tq):
    b, t, _ = q.shape
    in_specs = [pl.BlockSpec((1, tq, 2 * LANES), lambda bi, g, i: (bi, i, g))]
    args = [q]
    for arr in kv_sources:
        in_specs.append(pl.BlockSpec((1, 1) + arr.shape[2:], lambda bi, g, i: (bi, g, 0, 0)))
        args.append(arr)
    return pl.pallas_call(
        functools.partial(_attn_kernel, n_src=len(kv_sources) // 2),
        grid=(b, N_KV_HEADS, t // tq),
        in_specs=in_specs,
        out_specs=pl.BlockSpec((1, tq, 2 * LANES), lambda bi, g, i: (bi, i, g)),
        out_shape=jax.ShapeDtypeStruct((b, t, ATTN_WIDTH), bf16),
        compiler_params=_params(3),
        name="attention_latent" if len(kv_sources) > 2 else "attention_context",
    )(*args)


def _mid_kernel(x_ref, xp_ref, xn_ref, o_ref, op_ref, on_ref, g_ref, gp_ref, gn_ref,
                mod0_ref, mod1_ref, wout_ref, nw_ref, wzx_ref, wdt_ref, cw_ref, cb_ref, dtb_ref,
                alog_ref, lmat_ref, umat_ref,
                x1_ref, z_ref, xs_ref, bc_ref, acs_ref, rowt_ref, w_ref, lsum_ref):
    i = pl.program_id(1)
    last = pl.num_programs(1) - 1
    tm = x_ref.shape[1]
    rows = tm + 2 * HALO
    body = slice(HALO, HALO + tm)
    cat = lambda a, b_, c: jnp.concatenate([a[0], b_[0], c[0]], axis=0)
    gated = cat(op_ref, o_ref, on_ref) * cat(gp_ref, g_ref, gn_ref)
    x1 = cat(xp_ref, x_ref, xn_ref) + mod0_ref[0][:, 2 * D_MODEL:] * _dot(gated, wout_ref[...])
    x1_ref[0] = x1[body]
    mod = mod1_ref[0]
    h = _modulated_norm(x1, nw_ref[...], mod[:, :D_MODEL], mod[:, D_MODEL:2 * D_MODEL])
    h_body = h[body].astype(bf16)
    h = jnp.concatenate([jnp.where(i == 0, 0.0, h[:HALO]).astype(bf16), h_body,
                         jnp.where(i == last, 0.0, h[HALO + tm:]).astype(bf16)], axis=0)

    n_z = D_INNER // PROJ_BLOCK
    tasks = []
    for j in range(CONV_DIM // PROJ_BLOCK):
        tasks.append(("conv", j))
        if j < n_z:
            tasks.append(("z", j))

    def project(task):
        kind, j = task
        if kind == "z":
            return _dot(h_body, wzx_ref[:, PROJ_BLOCK * j:PROJ_BLOCK * (j + 1)])
        return _dot(h, wzx_ref[:, D_INNER + PROJ_BLOCK * j:D_INNER + PROJ_BLOCK * (j + 1)])

    queue = [project(t) for t in tasks[:PROJ_LOOKAHEAD]]
    for idx, (kind, j) in enumerate(tasks):
        blk = queue.pop(0)
        if idx + PROJ_LOOKAHEAD < len(tasks):
            queue.append(project(tasks[idx + PROJ_LOOKAHEAD]))
        cols = slice(PROJ_BLOCK * j, PROJ_BLOCK * (j + 1))
        if kind == "z":
            z_ref[0, :, cols] = blk.astype(bf16)
            continue
        act = _silu(cw_ref[0:1, cols] * pltpu.roll(blk, 1, 0)[body] + cw_ref[1:2, cols] * blk[body]
                    + cw_ref[2:3, cols] * pltpu.roll(blk, rows - 1, 0)[body]
                    + cb_ref[:, cols]).astype(bf16)
        if j < n_z:
            xs_ref[0, :, cols] = act
        else:
            bc_ref[0, :, PROJ_BLOCK * (j - n_z):PROJ_BLOCK * (j - n_z + 1)] = act

    dt = _softplus(_dot(h_body, wdt_ref[...]) + dtb_ref[...])
    a = -jnp.exp(alog_ref[...])
    fwd_lane = lax.broadcasted_iota(jnp.int32, (CHUNK, LANES), 1) < SSD_HEADS
    lower, upper = lmat_ref[...], umat_ref[...]
    for c in range(tm // CHUNK):
        dtc = dt[CHUNK * c:CHUNK * (c + 1)]
        ah, al = _split(dtc * a)
        acs = jnp.where(fwd_lane, _dot(lower, ah) + _dot(lower, al), _dot(upper, ah) + _dot(upper, al))
        end_row = jnp.where(fwd_lane[0:1], acs[CHUNK - 1:CHUNK], acs[0:1])
        acs_ref[0, CHUNK * c:CHUNK * (c + 1)] = acs
        w_ref[0, CHUNK * c:CHUNK * (c + 1)] = dtc * jnp.exp(end_row - acs)
        rowt_ref[0, c] = (acs - jnp.log(dtc)).T
        lsum_ref[0, c] = jnp.log(dtc + pltpu.roll(dtc, LANES - SSD_HEADS, 1)).T


def _mid(x, o, gs, mod0, mod1, w_out0, norm_w, w_zx, w_dt, conv_w, conv_b, dt_bias, alog, lmat, umat,
         *, latent, tm):
    b, t, _ = x.shape
    row = (lambda bi, i: (bi + 1, 0, 0)) if latent else (lambda bi, i: (0, 0, 0))
    const = lambda bi, i: (0, 0)
    tok = lambda bi, i: (bi, i, 0)
    hb = tm // HALO
    n_hb = t // HALO
    prev_map = lambda bi, i: (bi, jnp.maximum(i * hb - 1, 0), 0)
    next_map = lambda bi, i: (bi, jnp.minimum((i + 1) * hb, n_hb - 1), 0)
    halo3 = lambda width: [pl.BlockSpec((1, tm, width), tok),
                           pl.BlockSpec((1, HALO, width), prev_map),
                           pl.BlockSpec((1, HALO, width), next_map)]
    resident = lambda shape: pl.BlockSpec(shape, const, pipeline_mode=pl.Buffered(1))
    chunk4 = lambda bi, i: (bi, i, 0, 0)
    ncs = tm // CHUNK
    return pl.pallas_call(
        _mid_kernel,
        grid=(b, t // tm),
        in_specs=halo3(D_MODEL) + halo3(ATTN_WIDTH) + halo3(ATTN_WIDTH) + [
            pl.BlockSpec((1, 1, 3 * D_MODEL), row),
            pl.BlockSpec((1, 1, 3 * D_MODEL), row),
            resident((ATTN_WIDTH, D_MODEL)),
            pl.BlockSpec((1, D_MODEL), const),
            resident((D_MODEL, D_INNER + CONV_DIM)),
            resident((D_MODEL, LANES)),
            pl.BlockSpec((3, CONV_DIM), const),
            pl.BlockSpec((1, CONV_DIM), const),
            pl.BlockSpec((1, LANES), const),
            pl.BlockSpec((1, LANES), const),
            pl.BlockSpec((CHUNK, CHUNK), const),
            pl.BlockSpec((CHUNK, CHUNK), const)],
        out_specs=[pl.BlockSpec((1, tm, D_MODEL), tok),
                   pl.BlockSpec((1, tm, D_INNER), tok),
                   pl.BlockSpec((1, tm, D_INNER), tok),
                   pl.BlockSpec((1, tm, BC_WIDTH), tok),
                   pl.BlockSpec((1, tm, LANES), tok),
                   pl.BlockSpec((1, ncs, LANES, CHUNK), chunk4),
                   pl.BlockSpec((1, tm, LANES), tok),
                   pl.BlockSpec((1, ncs, LANES, CHUNK), chunk4)],
        out_shape=[jax.ShapeDtypeStruct((b, t, D_MODEL), f32),
                   jax.ShapeDtypeStruct((b, t, D_INNER), bf16),
                   jax.ShapeDtypeStruct((b, t, D_INNER), bf16),
                   jax.ShapeDtypeStruct((b, t, BC_WIDTH), bf16),
                   jax.ShapeDtypeStruct((b, t, LANES), f32),
                   jax.ShapeDtypeStruct((b, t // CHUNK, LANES, CHUNK), f32),
                   jax.ShapeDtypeStruct((b, t, LANES), f32),
                   jax.ShapeDtypeStruct((b, t // CHUNK, LANES, CHUNK), f32)],
        compiler_params=_params(2),
        name="mid_latent" if latent else "mid_context",
    )(x, x, x, o, o, o, gs, gs, gs, mod0, mod1, w_out0, norm_w, w_zx, w_dt, conv_w, conv_b,
      dt_bias, alog, lmat, umat)


def _init_state(ht_ref, h0_ref):
    if h0_ref is None:
        ht_ref[...] = jnp.zeros_like(ht_ref)
    else:
        for j in range(SSD_HEADS // 2):
            ht_ref[:, LANES * j:LANES * (j + 1)] = h0_ref[0, LANES * j:LANES * (j + 1), :].T


def _write_state(hfin_ref, ht_ref):
    for j in range(SSD_HEADS // 2):
        hfin_ref[0, LANES * j:LANES * (j + 1), :] = ht_ref[:, LANES * j:LANES * (j + 1)].T


def _lane_expand_decay(acs_row, e_ref):
    hi, lo = _split(jnp.broadcast_to(jnp.exp(acs_row), (HALF_VREG_ROWS, LANES)))
    return (_dot(hi, e_ref[...]) + _dot(lo, e_ref[...]))[0:1]


def _weighted_x(xs_ref, w_ref, e_ref, rows, g):
    gcols = slice(GROUP_W * g, GROUP_W * (g + 1))
    return xs_ref[0, rows, gcols] * _dot(w_ref[0, rows, :].astype(bf16), e_ref[:, gcols]).astype(bf16)


def _update_state(ht_ref, bc_ref, rows, g, xw, chunk_decay):
    gcols = slice(GROUP_W * g, GROUP_W * (g + 1))
    bm = bc_ref[0, rows, D_STATE * g:D_STATE * (g + 1)]
    st = _dot(bm.astype(f32).T.astype(bf16), xw)
    ht_ref[:, gcols] = ht_ref[:, gcols] * chunk_decay[:, gcols] + st


def _ssd_states_kernel(*refs, has_h0, final_state):
    xs_ref, bc_ref, acs_ref, w_ref, e_ref = refs[:5]
    refs = refs[5:]
    h0_ref = None
    if has_h0:
        h0_ref, refs = refs[0], refs[1:]
    hprev_ref, refs = refs[0], refs[1:]
    if final_state:
        hfin_ref, refs = refs[0], refs[1:]
    ht_ref = refs[0]
    step = pl.program_id(1)

    @pl.when(step == 0)
    def _init():
        _init_state(ht_ref, h0_ref)

    n_chunks = xs_ref.shape[1] // CHUNK
    xws = {(k, g): _weighted_x(xs_ref, w_ref, e_ref, slice(CHUNK * k, CHUNK * (k + 1)), g)
           for k in range(n_chunks) for g in range(SSD_GROUPS)}
    for k in range(n_chunks):
        rows = slice(CHUNK * k, CHUNK * (k + 1))
        chunk_decay = _lane_expand_decay(acs_ref[0, CHUNK * (k + 1) - 1:CHUNK * (k + 1), :], e_ref)
        hprev_ref[0, k] = ht_ref[...].astype(bf16)
        for g in range(SSD_GROUPS):
            _update_state(ht_ref, bc_ref, rows, g, xws[k, g], chunk_decay)

    if final_state:
        @pl.when(step == pl.num_programs(1) - 1)
        def _fin():
            _write_state(hfin_ref, ht_ref)


def _ssd_main_kernel(*refs, has_h0, final_state):
    xs_ref, bc_ref, acs_ref, rowt_ref, lsum_ref, w_ref, e_ref, hprev_ref = refs[:8]
    refs = refs[8:]
    h0_ref = None
    if has_h0:
        h0_ref, refs = refs[0], refs[1:]
    z_ref, x1_ref, mod_ref, gw_ref, wout_ref, dskip_ref = refs[:6]
    out_ref, refs = refs[6], refs[7:]
    if final_state:
        hfin_ref, refs = refs[0], refs[1:]
    ht_ref, y_ref, yn_ref = refs
    step = pl.program_id(1)

    @pl.when(step == 0)
    def _init():
        _init_state(ht_ref, h0_ref)

    pairs_per_group = HEADS_PER_GROUP // 2
    n_chunks = xs_ref.shape[1] // CHUNK
    ri = lax.broadcasted_iota(jnp.int32, (CHUNK, CHUNK), 0)
    ci = lax.broadcasted_iota(jnp.int32, (CHUNK, CHUNK), 1)
    past, future = ri > ci, ri < ci
    low = lax.broadcasted_iota(jnp.int32, (CHUNK, LANES), 1) < SSD_HEAD_DIM
    chunk_order = list(range(n_chunks - 1, -1, -1))

    cbs, xws = {}, {}
    for k in chunk_order:
        rows = slice(CHUNK * k, CHUNK * (k + 1))
        for g in range(SSD_GROUPS):
            bm = bc_ref[0, rows, D_STATE * g:D_STATE * (g + 1)]
            cm = bc_ref[0, rows, D_STATE * (SSD_GROUPS + g):D_STATE * (SSD_GROUPS + g + 1)]
            cbs[k, g] = _dot_nt(cm, bm)
            xws[k, g] = _weighted_x(xs_ref, w_ref, e_ref, rows, g)

    for k in chunk_order:
        rows = slice(CHUNK * k, CHUNK * (k + 1))
        acs = acs_ref[0, rows, :]
        rowt = rowt_ref[0, k]
        lsum = lsum_ref[0, k]
        chunk_decay = _lane_expand_decay(acs[0:1, :], e_ref)
        for g in range(SSD_GROUPS):
            cm_f = bc_ref[0, rows, D_STATE * (SSD_GROUPS + g):D_STATE * (SSD_GROUPS + g + 1)].astype(f32)
            cb = cbs[k, g]
            for jp in range(pairs_per_group):
                j = g * pairs_per_group + jp
                cols = slice(LANES * j, LANES * (j + 1))
                rhs = jnp.concatenate([xs_ref[0, rows, cols], hprev_ref[0, k, :, cols],
                                       ht_ref[:, cols].astype(bf16)], axis=0)
                ys = []
                for par in range(2):
                    hf = 2 * j + par
                    hb = SSD_HEADS + hf
                    col_f = jnp.broadcast_to(acs[:, hf:hf + 1], (CHUNK, CHUNK))
                    col_b = jnp.broadcast_to(acs[:, hb:hb + 1], (CHUNK, CHUNK))
                    expo = jnp.where(past, col_f - rowt[hf:hf + 1, :],
                                     jnp.where(future, col_b - rowt[hb:hb + 1, :], lsum[hf:hf + 1, :]))
                    lhs = jnp.concatenate([(cb * jnp.exp(expo)).astype(bf16),
                                           (cm_f * jnp.exp(col_f)).astype(bf16),
                                           (cm_f * jnp.exp(col_b)).astype(bf16)], axis=1)
                    ys.append(_dot(lhs, rhs))
                y_ref[:, cols] = jnp.where(low, ys[0], ys[1])
            _update_state(ht_ref, bc_ref, rows, g, xws[k, g], chunk_decay)

        y = y_ref[...] + dskip_ref[...] * xs_ref[0, rows, :].astype(f32)
        yz = y * _silu(z_ref[0, rows, :].astype(f32))
        ms = jnp.mean(yz * yz, axis=-1, keepdims=True)
        yn_ref[rows, :] = (yz * lax.rsqrt(ms + EPS) * gw_ref[...]).astype(bf16)

    gate = mod_ref[0][:, 2 * D_MODEL:]
    out_ref[0] = x1_ref[0] + gate * _dot(yn_ref[...], wout_ref[...])

    if final_state:
        @pl.when(step == pl.num_programs(1) - 1)
        def _fin():
            _write_state(hfin_ref, ht_ref)


def _ssd_states(xs, bc, acs, w, e_fwd, h0, *, latent):
    b, t, _ = xs.shape
    step_chunks = min(SSD_STEP_CHUNKS, t // CHUNK)
    rows = step_chunks * CHUNK
    blk = lambda bi, c: (bi, c, 0)
    per_b = lambda bi, c: (bi, 0, 0)
    in_specs = [pl.BlockSpec((1, rows, D_INNER), blk),
                pl.BlockSpec((1, rows, BC_WIDTH), blk),
                pl.BlockSpec((1, rows, LANES), blk),
                pl.BlockSpec((1, rows, LANES), blk),
                pl.BlockSpec((LANES, D_INNER), lambda bi, c: (0, 0))]
    args = [xs, bc, acs, w, e_fwd]
    if h0 is not None:
        in_specs.append(pl.BlockSpec((1, D_INNER, D_STATE), per_b))
        args.append(h0)
    out_specs = [pl.BlockSpec((1, step_chunks, D_STATE, D_INNER), lambda bi, c: (bi, c, 0, 0))]
    out_shape = [jax.ShapeDtypeStruct((b, t // CHUNK, D_STATE, D_INNER), bf16)]
    if not latent:
        out_specs.append(pl.BlockSpec((1, D_INNER, D_STATE), per_b))
        out_shape.append(jax.ShapeDtypeStruct((b, D_INNER, D_STATE), f32))
    return pl.pallas_call(
        functools.partial(_ssd_states_kernel, has_h0=h0 is not None, final_state=not latent),
        grid=(b, t // rows),
        in_specs=in_specs, out_specs=out_specs, out_shape=out_shape,
        scratch_shapes=[pltpu.VMEM((D_STATE, D_INNER), f32)],
        compiler_params=_params(2),
        name="ssd_states",
    )(*args)


def _ssd_main(xs, bc, acs, rowt, lsum, w, e_bwd, hprev, h0, z, x1, mod, gw, wout, dskip, *, latent):
    b, t, _ = xs.shape
    step_chunks = min(SSD_STEP_CHUNKS, t // CHUNK)
    rows = step_chunks * CHUNK
    n_steps = t // rows
    blk = lambda bi, c: (bi, n_steps - 1 - c, 0)
    blk4 = lambda bi, c: (bi, n_steps - 1 - c, 0, 0)
    const = lambda bi, c: (0, 0)
    per_b = lambda bi, c: (bi, 0, 0)
    row = (lambda bi, c: (bi + 1, 0, 0)) if latent else (lambda bi, c: (0, 0, 0))
    in_specs = [pl.BlockSpec((1, rows, D_INNER), blk),
                pl.BlockSpec((1, rows, BC_WIDTH), blk),
                pl.BlockSpec((1, rows, LANES), blk),
                pl.BlockSpec((1, step_chunks, LANES, CHUNK), blk4),
                pl.BlockSpec((1, step_chunks, LANES, CHUNK), blk4),
                pl.BlockSpec((1, rows, LANES), blk),
                pl.BlockSpec((LANES, D_INNER), const),
                pl.BlockSpec((1, step_chunks, D_STATE, D_INNER), blk4)]
    args = [xs, bc, acs, rowt, lsum, w, e_bwd, hprev]
    if h0 is not None:
        in_specs.append(pl.BlockSpec((1, D_INNER, D_STATE), per_b))
        args.append(h0)
    in_specs += [pl.BlockSpec((1, rows, D_INNER), blk),
                 pl.BlockSpec((1, rows, D_MODEL), blk),
                 pl.BlockSpec((1, 1, 3 * D_MODEL), row),
                 pl.BlockSpec((1, D_INNER), const),
                 pl.BlockSpec((D_INNER, D_MODEL), const),
                 pl.BlockSpec((1, D_INNER), const)]
    args += [z, x1, mod, gw, wout, dskip]
    out_specs = [pl.BlockSpec((1, rows, D_MODEL), blk)]
    out_shape = [jax.ShapeDtypeStruct((b, t, D_MODEL), f32)]
    if not latent:
        out_specs.append(pl.BlockSpec((1, D_INNER, D_STATE), per_b))
        out_shape.append(jax.ShapeDtypeStruct((b, D_INNER, D_STATE), f32))
    return pl.pallas_call(
        functools.partial(_ssd_main_kernel, has_h0=h0 is not None, final_state=not latent),
        grid=(b, n_steps),
        in_specs=in_specs, out_specs=out_specs, out_shape=out_shape,
        scratch_shapes=[pltpu.VMEM((D_STATE, D_INNER), f32), pltpu.VMEM((CHUNK, D_INNER), f32),
                        pltpu.VMEM((rows, D_INNER), bf16)],
        compiler_params=_params(2),
        name="ssd_main",
    )(*args)


def _rope_tables(n_tokens):
    rows = n_tokens // GRID_W
    row_ids = jnp.repeat(jnp.arange(rows), GRID_W).astype(f32)
    col_ids = jnp.tile(jnp.arange(GRID_W), rows).astype(f32)
    inv_freq = 1.0 / (ROPE_THETA ** (jnp.arange(0, AXIS_DIM, 2, dtype=f32) / AXIS_DIM))
    ang = jnp.stack([row_ids[:, None] * inv_freq, col_ids[:, None] * inv_freq], axis=1)
    cos, sin, zero = jnp.cos(ang), jnp.sin(ang), jnp.zeros_like(ang)
    head = lambda a, b_: jnp.stack([a, b_], axis=2).reshape(n_tokens, HEAD_DIM)
    two = lambda u: jnp.concatenate([u, u], axis=1)
    return two(head(cos, cos)), two(head(-sin, zero)), two(head(zero, sin))


def kernel(x_prompt, x_sample, cache_k_l0, cache_v_l0, state_fwd_l1, state_bwd_l1, c, c_ctx,
           l0_norm_w, l0_mod_w, l0_mod_b, l0_w_in, l0_q_norm, l0_k_norm, l0_w_out,
           l1_norm_w, l1_mod_w, l1_mod_b, l1_w_in, l1_conv_w, l1_conv_b, l1_dt_bias_f, l1_dt_bias_b,
           l1_a_log_f, l1_a_log_b, l1_d_skip, l1_gnorm_w, l1_w_out):
    dec_b = x_sample.shape[0]
    pad = lambda u, n: jnp.pad(u, (0, n - u.shape[0]))

    n_rows = 16
    cond = jnp.zeros((n_rows, D_MODEL), f32).at[0].set(c_ctx).at[1:1 + dec_b].set(c)
    mod0 = _ada_mod(cond, l0_mod_w, l0_mod_b).reshape(n_rows, 1, 3 * D_MODEL)
    mod1 = _ada_mod(cond, l1_mod_w, l1_mod_b).reshape(n_rows, 1, 3 * D_MODEL)

    q_scale = HEAD_DIM ** -0.5 * math.log2(math.e)
    qn = (jnp.tile(l0_q_norm, N_HEADS) * q_scale).reshape(1, ATTN_WIDTH)
    kn = jnp.tile(l0_k_norm, N_KV_HEADS).reshape(1, KV_WIDTH)
    head_of = jnp.arange(ATTN_WIDTH) // HEAD_DIM
    e = (head_of[:, None] == jnp.arange(LANES)[None, :]).astype(bf16)
    et = e.T
    w_in0 = l0_w_in.astype(bf16)
    w_out0 = l0_w_out.astype(bf16)
    w_zx = l1_w_in[:, :D_INNER + CONV_DIM].astype(bf16)
    w_dt = jnp.pad(l1_w_in[:, D_INNER + CONV_DIM:], ((0, 0), (0, LANES - 2 * SSD_HEADS))).astype(bf16)
    w_out1 = l1_w_out.astype(bf16)
    dt_bias = pad(jnp.concatenate([l1_dt_bias_f, l1_dt_bias_b]), LANES).reshape(1, LANES)
    alog = pad(jnp.concatenate([l1_a_log_f, l1_a_log_b]), LANES).reshape(1, LANES)
    dskip = jnp.repeat(l1_d_skip, SSD_HEAD_DIM).reshape(1, D_INNER)
    nw0 = l0_norm_w.reshape(1, D_MODEL)
    nw1 = l1_norm_w.reshape(1, D_MODEL)
    gw = l1_gnorm_w.reshape(1, D_INNER)
    conv_b = l1_conv_b.reshape(1, CONV_DIM)
    tri = jnp.arange(CHUNK)
    lmat = (tri[None, :] <= tri[:, None]).astype(bf16)
    umat = lmat.T
    rope_tabs = _rope_tables(x_sample.shape[1])
    ssd_head_of = jnp.arange(D_INNER) // SSD_HEAD_DIM
    e_fwd = (jnp.arange(LANES)[:, None] == ssd_head_of[None, :]).astype(bf16)
    e_bwd = (jnp.arange(LANES)[:, None] == SSD_HEADS + ssd_head_of[None, :]).astype(bf16)

    kc = cache_k_l0.transpose(0, 2, 1, 3).astype(bf16)
    vc = cache_v_l0.transpose(0, 2, 3, 1).astype(bf16)
    kk_ctx = jnp.concatenate([kc, kc], axis=-1)
    vt_ctx = jnp.concatenate(
        [vc, jnp.ones(vc.shape[:2] + (VT_ROWS - HEAD_DIM, vc.shape[3]), bf16)], axis=2)
    h0_f = state_fwd_l1.reshape(dec_b, D_INNER, D_STATE)
    h0_b = state_bwd_l1.reshape(dec_b, D_INNER, D_STATE)

    outs = {}
    for latent, x in ((False, x_prompt), (True, x_sample)):
        res = _l0_inproj(x, mod0, nw0, w_in0, qn, kn, e, et, rope_tabs, latent=latent,
                         tm=2 * L0_SUB_ROWS if latent else L0_SUB_ROWS)
        if latent:
            q, kk, vt, gs = res
            kv = [kk_ctx, vt_ctx, kk, vt]
        else:
            q, kk, vt, gs, k_new, v_new = res
            kv = [kk, vt]
        o = _attention(q, kv, tq=4 * Q_SUB_ROWS if latent else Q_SUB_ROWS)
        x1, z, xs, bc, acs, rowt, w, lsum = _mid(
            x, o, gs, mod0, mod1, w_out0, nw1, w_zx, w_dt, l1_conv_w, conv_b, dt_bias, alog,
            lmat, umat, latent=latent, tm=512 if latent else 256)
        fwd = _ssd_states(xs, bc, acs, w, e_fwd, h0_f if latent else None, latent=latent)
        bwd = _ssd_main(xs, bc, acs, rowt, lsum, w, e_bwd, fwd[0], h0_b if latent else None,
                        z, x1, mod1, gw, w_out1, dskip, latent=latent)
        if latent:
            outs["y_sample"] = bwd[0]
        else:
            b = x.shape[0]
            outs["y_prompt"] = bwd[0]
            outs["k"] = k_new.reshape(b, -1, N_KV_HEADS, HEAD_DIM)
            outs["v"] = v_new.reshape(b, -1, N_KV_HEADS, HEAD_DIM)
            outs["hf"] = fwd[1].reshape(b, SSD_HEADS, SSD_HEAD_DIM, D_STATE)
            outs["hb"] = bwd[1].reshape(b, SSD_HEADS, SSD_HEAD_DIM, D_STATE)
    return (outs["y_prompt"], outs["y_sample"], outs["k"], outs["v"], outs["hf"], outs["hb"])
```

```python
import functools
import math

import jax
import jax.numpy as jnp
from jax import lax
from jax.experimental import pallas as pl
from jax.experimental.pallas import tpu as pltpu

f32 = jnp.float32
bf16 = jnp.bfloat16

D_MODEL = 1024
GRID_W = 64
EPS = 1e-6
N_HEADS = 16
N_KV_HEADS = 4
HEAD_DIM = 64
ATTN_WIDTH = N_HEADS * HEAD_DIM
KV_WIDTH = N_KV_HEADS * HEAD_DIM
AXIS_DIM = HEAD_DIM // 2
ROPE_THETA = 10000.0
D_INNER = 2048
SSD_HEAD_DIM = 64
SSD_HEADS = D_INNER // SSD_HEAD_DIM
SSD_GROUPS = 4
HEADS_PER_GROUP = SSD_HEADS // SSD_GROUPS
D_STATE = 128
CHUNK = 128
BC_WIDTH = 2 * SSD_GROUPS * D_STATE
CONV_DIM = D_INNER + BC_WIDTH

LANES = 128
BF16_ROWS = 16
HALO = BF16_ROWS
PROJ_BLOCK = 256
PROJ_LOOKAHEAD = 2
L0_SUB_ROWS = 256
SSD_STEP_CHUNKS = 4
HALF_VREG_ROWS = 8
GROUP_W = HEADS_PER_GROUP * SSD_HEAD_DIM
VT_ROWS = HEAD_DIM + BF16_ROWS
Q_SUB_ROWS = 256
KEY_CHUNK = 256
SCORE_LOOKAHEAD = 8
VMEM_LIMIT = 56 * 1024 * 1024

LOG2_E = math.log2(math.e)


def _params(n_axes):
    return pltpu.CompilerParams(
        dimension_semantics=("arbitrary",) * n_axes, vmem_limit_bytes=VMEM_LIMIT)


def _dot(a, b):
    return jnp.dot(a, b, preferred_element_type=f32)


def _dot_nt(a, b):
    return lax.dot_general(a, b, (((1,), (1,)), ((), ())), preferred_element_type=f32)


def _split(x):
    hi = x.astype(bf16)
    lo = (x - hi.astype(f32)).astype(bf16)
    return hi, lo


def _silu(x):
    return x * jax.nn.sigmoid(x)


def _softplus(x):
    return jnp.maximum(x, 0.0) + jnp.log1p(jnp.exp(-jnp.abs(x)))


def _modulated_norm(x, norm_w, shift, scale):
    ms = jnp.mean(x * x, axis=-1, keepdims=True)
    return (x * lax.rsqrt(ms + EPS) * norm_w) * (1.0 + scale) + shift


def _ada_mod_kernel(cond_ref, w_ref, b_ref, o_ref):
    s = _silu(cond_ref[...])
    sh, sl = _split(s)
    wh, wl = _split(w_ref[...])
    o_ref[...] = _dot(sh, wh) + (_dot(sh, wl) + _dot(sl, wh)) + b_ref[...]


def _ada_mod(cond, mod_w, mod_b):
    rows = cond.shape[0]
    tn = 1024
    return pl.pallas_call(
        _ada_mod_kernel,
        grid=(3 * D_MODEL // tn,),
        in_specs=[pl.BlockSpec((rows, D_MODEL), lambda j: (0, 0)),
                  pl.BlockSpec((D_MODEL, tn), lambda j: (0, j)),
                  pl.BlockSpec((1, tn), lambda j: (0, j))],
        out_specs=pl.BlockSpec((rows, tn), lambda j: (0, j)),
        out_shape=jax.ShapeDtypeStruct((rows, 3 * D_MODEL), f32),
        compiler_params=_params(1),
        name="ada_mod",
    )(cond, mod_w, mod_b.reshape(1, -1))


def _head_rms(t, e, et, w):
    hi, lo = _split(t * t)
    ss = _dot(hi, e) + _dot(lo, e)
    rh, rl = _split(lax.rsqrt(ss * (1.0 / HEAD_DIM) + EPS))
    return t * (_dot(rh, et) + _dot(rl, et)) * w


def _rope(t, cos, sin_a, sin_b):
    width = t.shape[1]
    rep = width // LANES
    tile = lambda u: jnp.concatenate([u] * rep, axis=1)
    return (t * tile(cos) + pltpu.roll(t, width - AXIS_DIM // 2, 1) * tile(sin_a)
            + pltpu.roll(t, AXIS_DIM // 2, 1) * tile(sin_b))


def _l0_inproj_kernel(*refs, latent):
    x_ref, mod_ref, nw_ref, w_ref, qn_ref, kn_ref, e_ref, et_ref = refs[:8]
    refs = refs[8:]
    if latent:
        cos_ref, sa_ref, sb_ref = refs[:3]
        q_ref, kk_ref, vt_ref, gs_ref = refs[3:]
    else:
        q_ref, kk_ref, vt_ref, gs_ref, knew_ref, vnew_ref = refs
    mod = mod_ref[0]
    e = e_ref[...]
    et = et_ref[...]
    subs = [slice(r0, r0 + L0_SUB_ROWS) for r0 in range(0, x_ref.shape[1], L0_SUB_ROWS)]
    projs = []
    for rows in subs:
        h = _modulated_norm(x_ref[0, rows, :], nw_ref[...], mod[:, :D_MODEL], mod[:, D_MODEL:2 * D_MODEL])
        projs.append(_dot(h.astype(bf16), w_ref[...]))
    low = lax.broadcasted_iota(jnp.int32, (L0_SUB_ROWS, LANES), 1) < HEAD_DIM
    ones = jnp.ones((VT_ROWS - HEAD_DIM, L0_SUB_ROWS), bf16)
    for rows, proj in zip(subs, projs):
        q = proj[:, :ATTN_WIDTH]
        k = proj[:, ATTN_WIDTH:ATTN_WIDTH + KV_WIDTH]
        v = proj[:, ATTN_WIDTH + KV_WIDTH:ATTN_WIDTH + 2 * KV_WIDTH]
        g = proj[:, ATTN_WIDTH + 2 * KV_WIDTH:]
        q = _head_rms(q, e, et, qn_ref[...])
        k = _head_rms(k, e[:KV_WIDTH], et[:, :KV_WIDTH], kn_ref[...])
        if latent:
            cos, sa, sb = cos_ref[rows, :], sa_ref[rows, :], sb_ref[rows, :]
            q = _rope(q, cos, sa, sb)
            k = _rope(k, cos, sa, sb)
        else:
            knew_ref[0, rows, :] = k
            vnew_ref[0, rows, :] = v
        q_ref[0, rows, :] = q.astype(bf16)
        gs_ref[0, rows, :] = _silu(g).astype(bf16)
        v_t = v.T
        for j in range(KV_WIDTH // LANES):
            kb = k[:, LANES * j:LANES * (j + 1)]
            kbs = pltpu.roll(kb, HEAD_DIM, 1)
            kk_ref[0, 2 * j, rows, :] = jnp.where(low, kb, kbs).astype(bf16)
            kk_ref[0, 2 * j + 1, rows, :] = jnp.where(low, kbs, kb).astype(bf16)
        for kvh in range(N_KV_HEADS):
            vt_ref[0, kvh, :HEAD_DIM, rows] = v_t[HEAD_DIM * kvh:HEAD_DIM * (kvh + 1)].astype(bf16)
            vt_ref[0, kvh, HEAD_DIM:, rows] = ones


def _l0_inproj(x, mod, norm_w, w_in, qn, kn, e, et, rope_tabs, *, latent, tm):
    b, t, _ = x.shape
    attn_in = w_in.shape[1]
    row = (lambda bi, i: (bi + 1, 0, 0)) if latent else (lambda bi, i: (0, 0, 0))
    const = lambda bi, i: (0, 0)
    in_specs = [pl.BlockSpec((1, tm, D_MODEL), lambda bi, i: (bi, i, 0)),
                pl.BlockSpec((1, 1, 3 * D_MODEL), row),
                pl.BlockSpec((1, D_MODEL), const),
                pl.BlockSpec((D_MODEL, attn_in), const),
                pl.BlockSpec((1, ATTN_WIDTH), const),
                pl.BlockSpec((1, KV_WIDTH), const),
                pl.BlockSpec((ATTN_WIDTH, LANES), const),
                pl.BlockSpec((LANES, ATTN_WIDTH), const)]
    args = [x, mod, norm_w, w_in, qn, kn, e, et]
    tok = lambda bi, i: (bi, i, 0)
    out_specs = [pl.BlockSpec((1, tm, ATTN_WIDTH), tok),
                 pl.BlockSpec((1, N_KV_HEADS, tm, LANES), lambda bi, i: (bi, 0, i, 0)),
                 pl.BlockSpec((1, N_KV_HEADS, VT_ROWS, tm), lambda bi, i: (bi, 0, 0, i)),
                 pl.BlockSpec((1, tm, ATTN_WIDTH), tok)]
    out_shape = [jax.ShapeDtypeStruct((b, t, ATTN_WIDTH), bf16),
                 jax.ShapeDtypeStruct((b, N_KV_HEADS, t, LANES), bf16),
                 jax.ShapeDtypeStruct((b, N_KV_HEADS, VT_ROWS, t), bf16),
                 jax.ShapeDtypeStruct((b, t, ATTN_WIDTH), bf16)]
    if latent:
        in_specs += [pl.BlockSpec((tm, LANES), lambda bi, i: (i, 0))] * 3
        args += list(rope_tabs)
    else:
        out_specs += [pl.BlockSpec((1, tm, KV_WIDTH), tok)] * 2
        out_shape += [jax.ShapeDtypeStruct((b, t, KV_WIDTH), f32)] * 2
    return pl.pallas_call(
        functools.partial(_l0_inproj_kernel, latent=latent),
        grid=(b, t // tm),
        in_specs=in_specs, out_specs=out_specs, out_shape=out_shape,
        compiler_params=_params(2),
        name="l0_inproj_latent" if latent else "l0_inproj_context",
    )(*args)


def _attn_kernel(*refs, n_src):
    q_ref = refs[0]
    kv_refs = refs[1:1 + 2 * n_src]
    o_ref = refs[-1]
    n_sub = q_ref.shape[1] // Q_SUB_ROWS
    low = lax.broadcasted_iota(jnp.int32, (Q_SUB_ROWS, LANES), 1) < HEAD_DIM
    zero = jnp.zeros((Q_SUB_ROWS, LANES), bf16)
    chunks = []
    for i in range(n_src):
        n_keys = kv_refs[2 * i].shape[2]
        width = min(KEY_CHUNK, n_keys)
        chunks += [(kv_refs[2 * i], kv_refs[2 * i + 1], slice(c0, c0 + width))
                   for c0 in range(0, n_keys, width)]
    steps = [(sub, head, ch) for sub in range(n_sub) for head in range(4)
             for ch in range(len(chunks))]

    def q_masked(sub, head):
        qp = q_ref[0, Q_SUB_ROWS * sub:Q_SUB_ROWS * (sub + 1),
                   LANES * (head // 2):LANES * (head // 2 + 1)]
        return jnp.where(low, qp, zero) if head % 2 == 0 else jnp.where(low, zero, qp)

    def scores(step):
        sub, head, ch = step
        k_ref, _, keys = chunks[ch]
        return _dot_nt(k_ref[0, 0, keys, :], q_masked(sub, head))

    halves = []
    m = acc = None
    queue = [scores(st) for st in steps[:SCORE_LOOKAHEAD]]
    for idx, (sub, head, ch) in enumerate(steps):
        s = queue.pop(0)
        if idx + SCORE_LOOKAHEAD < len(steps):
            queue.append(scores(steps[idx + SCORE_LOOKAHEAD]))
        _, vt_ref, keys = chunks[ch]
        s = s.astype(bf16)
        slab = s.shape[0] // 4
        m_c = jnp.maximum(jnp.maximum(s[:slab], s[slab:2 * slab]),
                          jnp.maximum(s[2 * slab:3 * slab], s[3 * slab:]))
        m_c = jnp.max(m_c.astype(f32), axis=0, keepdims=True)
        m_new = m_c if ch == 0 else jnp.maximum(m, m_c)
        pv = _dot(vt_ref[0, 0, :, keys], jnp.exp2(s - m_new.astype(bf16)))
        acc = pv if ch == 0 else acc * jnp.exp2(m - m_new) + pv
        m = m_new
        if ch == len(chunks) - 1:
            halves.append(acc[:HEAD_DIM] / acc[HEAD_DIM:HEAD_DIM + 1])
            if head % 2 == 1:
                o_ref[0, Q_SUB_ROWS * sub:Q_SUB_ROWS * (sub + 1),
                      LANES * (head // 2):LANES * (head // 2 + 1)] = (
                    jnp.concatenate(halves, axis=0).T.astype(bf16))
                halves = []


def _attention(q, kv_sources, *, tq):
    b, t, _ = q.shape
    in_specs = [pl.BlockSpec((1, tq, 2 * LANES), lambda bi, g, i: (bi, i, g))]
    args = [q]
    for arr in kv_sources:
        in_specs.append(pl.BlockSpec((1, 1) + arr.shape[2:], lambda bi, g, i: (bi, g, 0, 0)))
        args.append(arr)
    return pl.pallas_call(
        functools.partial(_attn_kernel, n_src=len(kv_sources) // 2),
        grid=(b, N_KV_HEADS, t // tq),
        in_specs=in_specs,
        out_specs=pl.BlockSpec((1, tq, 2 * LANES), lambda bi, g, i: (bi, i, g)),
        out_shape=jax.ShapeDtypeStruct((b, t, ATTN_WIDTH), bf16),
        compiler_params=_params(3),
        name="attention_latent" if len(kv_sources) > 2 else "attention_context",
    )(*args)


def _mid_kernel(x_ref, xp_ref, xn_ref, o_ref, op_ref, on_ref, g_ref, gp_ref, gn_ref,
                mod0_ref, mod1_ref, wout_ref, nw_ref, wzx_ref, wdt_ref, cw_ref, cb_ref, dtb_ref,
                alog_ref, lmat_ref, umat_ref,
                x1_ref, z_ref, xs_ref, bc_ref, acs_ref, rowt_ref, w_ref, lsum_ref):
    i = pl.program_id(1)
    last = pl.num_programs(1) - 1
    tm = x_ref.shape[1]
    rows = tm + 2 * HALO
    body = slice(HALO, HALO + tm)
    cat = lambda a, b_, c: jnp.concatenate([a[0], b_[0], c[0]], axis=0)
    gated = cat(op_ref, o_ref, on_ref) * cat(gp_ref, g_ref, gn_ref)
    x1 = cat(xp_ref, x_ref, xn_ref) + mod0_ref[0][:, 2 * D_MODEL:] * _dot(gated, wout_ref[...])
    x1_ref[0] = x1[body]
    mod = mod1_ref[0]
    h = _modulated_norm(x1, nw_ref[...], mod[:, :D_MODEL], mod[:, D_MODEL:2 * D_MODEL])
    h_body = h[body].astype(bf16)
    h = jnp.concatenate([jnp.where(i == 0, 0.0, h[:HALO]).astype(bf16), h_body,
                         jnp.where(i == last, 0.0, h[HALO + tm:]).astype(bf16)], axis=0)

    n_z = D_INNER // PROJ_BLOCK
    tasks = []
    for j in range(CONV_DIM // PROJ_BLOCK):
        tasks.append(("conv", j))
        if j < n_z:
            tasks.append(("z", j))

    def project(task):
        kind, j = task
        if kind == "z":
            return _dot(h_body, wzx_ref[:, PROJ_BLOCK * j:PROJ_BLOCK * (j + 1)])
        return _dot(h, wzx_ref[:, D_INNER + PROJ_BLOCK * j:D_INNER + PROJ_BLOCK * (j + 1)])

    queue = [project(t) for t in tasks[:PROJ_LOOKAHEAD]]
    for idx, (kind, j) in enumerate(tasks):
        blk = queue.pop(0)
        if idx + PROJ_LOOKAHEAD < len(tasks):
            queue.append(project(tasks[idx + PROJ_LOOKAHEAD]))
        cols = slice(PROJ_BLOCK * j, PROJ_BLOCK * (j + 1))
        if kind == "z":
            z_ref[0, :, cols] = blk.astype(bf16)
            continue
        act = _silu(cw_ref[0:1, cols] * pltpu.roll(blk, 1, 0)[body] + cw_ref[1:2, cols] * blk[body]
                    + cw_ref[2:3, cols] * pltpu.roll(blk, rows - 1, 0)[body]
                    + cb_ref[:, cols]).astype(bf16)
        if j < n_z:
            xs_ref[0, :, cols] = act
        else:
            bc_ref[0, :, PROJ_BLOCK * (j - n_z):PROJ_BLOCK * (j - n_z + 1)] = act

    dt = _softplus(_dot(h_body, wdt_ref[...]) + dtb_ref[...])
    a = -jnp.exp(alog_ref[...])
    fwd_lane = lax.broadcasted_iota(jnp.int32, (CHUNK, LANES), 1) < SSD_HEADS
    lower, upper = lmat_ref[...], umat_ref[...]
    for c in range(tm // CHUNK):
        dtc = dt[CHUNK * c:CHUNK * (c + 1)]
        ah, al = _split(dtc * a)
        acs = jnp.where(fwd_lane, _dot(lower, ah) + _dot(lower, al), _dot(upper, ah) + _dot(upper, al))
        end_row = jnp.where(fwd_lane[0:1], acs[CHUNK - 1:CHUNK], acs[0:1])
        w_ref[0, CHUNK * c:CHUNK * (c + 1)] = dtc * jnp.exp(end_row - acs)
        acs2 = acs * LOG2_E
        acs_ref[0, CHUNK * c:CHUNK * (c + 1)] = acs2
        rowt_ref[0, c] = (acs2 - jnp.log2(dtc)).T
        lsum_ref[0, c] = jnp.log2(dtc + pltpu.roll(dtc, LANES - SSD_HEADS, 1)).T


def _mid(x, o, gs, mod0, mod1, w_out0, norm_w, w_zx, w_dt, conv_w, conv_b, dt_bias, alog, lmat, umat,
         *, latent, tm):
    b, t, _ = x.shape
    row = (lambda bi, i: (bi + 1, 0, 0)) if latent else (lambda bi, i: (0, 0, 0))
    const = lambda bi, i: (0, 0)
    tok = lambda bi, i: (bi, i, 0)
    hb = tm // HALO
    n_hb = t // HALO
    prev_map = lambda bi, i: (bi, jnp.maximum(i * hb - 1, 0), 0)
    next_map = lambda bi, i: (bi, jnp.minimum((i + 1) * hb, n_hb - 1), 0)
    halo3 = lambda width: [pl.BlockSpec((1, tm, width), tok),
                           pl.BlockSpec((1, HALO, width), prev_map),
                           pl.BlockSpec((1, HALO, width), next_map)]
    resident = lambda shape: pl.BlockSpec(shape, const, pipeline_mode=pl.Buffered(1))
    chunk4 = lambda bi, i: (bi, i, 0, 0)
    ncs = tm // CHUNK
    return pl.pallas_call(
        _mid_kernel,
        grid=(b, t // tm),
        in_specs=halo3(D_MODEL) + halo3(ATTN_WIDTH) + halo3(ATTN_WIDTH) + [
            pl.BlockSpec((1, 1, 3 * D_MODEL), row),
            pl.BlockSpec((1, 1, 3 * D_MODEL), row),
            resident((ATTN_WIDTH, D_MODEL)),
            pl.BlockSpec((1, D_MODEL), const),
            resident((D_MODEL, D_INNER + CONV_DIM)),
            resident((D_MODEL, LANES)),
            pl.BlockSpec((3, CONV_DIM), const),
            pl.BlockSpec((1, CONV_DIM), const),
            pl.BlockSpec((1, LANES), const),
            pl.BlockSpec((1, LANES), const),
            pl.BlockSpec((CHUNK, CHUNK), const),
            pl.BlockSpec((CHUNK, CHUNK), const)],
        out_specs=[pl.BlockSpec((1, tm, D_MODEL), tok),
                   pl.BlockSpec((1, tm, D_INNER), tok),
                   pl.BlockSpec((1, tm, D_INNER), tok),
                   pl.BlockSpec((1, tm, BC_WIDTH), tok),
                   pl.BlockSpec((1, tm, LANES), tok),
                   pl.BlockSpec((1, ncs, LANES, CHUNK), chunk4),
                   pl.BlockSpec((1, tm, LANES), tok),
                   pl.BlockSpec((1, ncs, LANES, CHUNK), chunk4)],
        out_shape=[jax.ShapeDtypeStruct((b, t, D_MODEL), f32),
                   jax.ShapeDtypeStruct((b, t, D_INNER), bf16),
                   jax.ShapeDtypeStruct((b, t, D_INNER), bf16),
                   jax.ShapeDtypeStruct((b, t, BC_WIDTH), bf16),
                   jax.ShapeDtypeStruct((b, t, LANES), f32),
                   jax.ShapeDtypeStruct((b, t // CHUNK, LANES, CHUNK), f32),
                   jax.ShapeDtypeStruct((b, t, LANES), f32),
                   jax.ShapeDtypeStruct((b, t // CHUNK, LANES, CHUNK), f32)],
        compiler_params=_params(2),
        name="mid_latent" if latent else "mid_context",
    )(x, x, x, o, o, o, gs, gs, gs, mod0, mod1, w_out0, norm_w, w_zx, w_dt, conv_w, conv_b,
      dt_bias, alog, lmat, umat)


def _init_state(ht_ref, h0_ref):
    if h0_ref is None:
        ht_ref[...] = jnp.zeros_like(ht_ref)
    else:
        for j in range(SSD_HEADS // 2):
            ht_ref[:, LANES * j:LANES * (j + 1)] = h0_ref[0, LANES * j:LANES * (j + 1), :].T


def _write_state(hfin_ref, ht_ref):
    for j in range(SSD_HEADS // 2):
        hfin_ref[0, LANES * j:LANES * (j + 1), :] = ht_ref[:, LANES * j:LANES * (j + 1)].T


def _lane_expand_decay(acs_row, e_ref):
    hi, lo = _split(jnp.broadcast_to(jnp.exp2(acs_row), (HALF_VREG_ROWS, LANES)))
    return (_dot(hi, e_ref[...]) + _dot(lo, e_ref[...]))[0:1]


def _weighted_x(xs_ref, w_ref, e_ref, rows, g):
    gcols = slice(GROUP_W * g, GROUP_W * (g + 1))
    return xs_ref[0, rows, gcols] * _dot(w_ref[0, rows, :].astype(bf16), e_ref[:, gcols]).astype(bf16)


def _update_state(ht_ref, bc_ref, rows, g, xw, chunk_decay):
    gcols = slice(GROUP_W * g, GROUP_W * (g + 1))
    bm = bc_ref[0, rows, D_STATE * g:D_STATE * (g + 1)]
    st = _dot(bm.astype(f32).T.astype(bf16), xw)
    ht_ref[:, gcols] = ht_ref[:, gcols] * chunk_decay[:, gcols] + st


def _ssd_states_kernel(*refs, has_h0, final_state):
    xs_ref, bc_ref, acs_ref, w_ref, e_ref = refs[:5]
    refs = refs[5:]
    h0_ref = None
    if has_h0:
        h0_ref, refs = refs[0], refs[1:]
    hprev_ref, refs = refs[0], refs[1:]
    if final_state:
        hfin_ref, refs = refs[0], refs[1:]
    ht_ref = refs[0]
    step = pl.program_id(1)

    @pl.when(step == 0)
    def _init():
        _init_state(ht_ref, h0_ref)

    n_chunks = xs_ref.shape[1] // CHUNK
    xws = {(k, g): _weighted_x(xs_ref, w_ref, e_ref, slice(CHUNK * k, CHUNK * (k + 1)), g)
           for k in range(n_chunks) for g in range(SSD_GROUPS)}
    for k in range(n_chunks):
        rows = slice(CHUNK * k, CHUNK * (k + 1))
        chunk_decay = _lane_expand_decay(acs_ref[0, CHUNK * (k + 1) - 1:CHUNK * (k + 1), :], e_ref)
        hprev_ref[0, k] = ht_ref[...].astype(bf16)
        for g in range(SSD_GROUPS):
            _update_state(ht_ref, bc_ref, rows, g, xws[k, g], chunk_decay)

    if final_state:
        @pl.when(step == pl.num_programs(1) - 1)
        def _fin():
            _write_state(hfin_ref, ht_ref)


def _ssd_main_kernel(*refs, has_h0, final_state):
    xs_ref, bc_ref, acs_ref, rowt_ref, lsum_ref, w_ref, e_ref, hprev_ref = refs[:8]
    refs = refs[8:]
    h0_ref = None
    if has_h0:
        h0_ref, refs = refs[0], refs[1:]
    z_ref, x1_ref, mod_ref, gw_ref, wout_ref, dskip_ref = refs[:6]
    out_ref, refs = refs[6], refs[7:]
    if final_state:
        hfin_ref, refs = refs[0], refs[1:]
    ht_ref, y_ref, yn_ref = refs
    step = pl.program_id(1)

    @pl.when(step == 0)
    def _init():
        _init_state(ht_ref, h0_ref)

    pairs_per_group = HEADS_PER_GROUP // 2
    n_chunks = xs_ref.shape[1] // CHUNK
    ri = lax.broadcasted_iota(jnp.int32, (CHUNK, CHUNK), 0)
    ci = lax.broadcasted_iota(jnp.int32, (CHUNK, CHUNK), 1)
    past, future = ri > ci, ri < ci
    low = lax.broadcasted_iota(jnp.int32, (CHUNK, LANES), 1) < SSD_HEAD_DIM
    chunk_order = list(range(n_chunks - 1, -1, -1))

    cbs, xws = {}, {}
    for k in chunk_order:
        rows = slice(CHUNK * k, CHUNK * (k + 1))
        for g in range(SSD_GROUPS):
            bm = bc_ref[0, rows, D_STATE * g:D_STATE * (g + 1)]
            cm = bc_ref[0, rows, D_STATE * (SSD_GROUPS + g):D_STATE * (SSD_GROUPS + g + 1)]
            cbs[k, g] = _dot_nt(cm, bm)
            xws[k, g] = _weighted_x(xs_ref, w_ref, e_ref, rows, g)

    for k in chunk_order:
        rows = slice(CHUNK * k, CHUNK * (k + 1))
        acs = acs_ref[0, rows, :]
        rowt = rowt_ref[0, k]
        lsum = lsum_ref[0, k]
        chunk_decay = _lane_expand_decay(acs[0:1, :], e_ref)
        for g in range(SSD_GROUPS):
            cm_f = bc_ref[0, rows, D_STATE * (SSD_GROUPS + g):D_STATE * (SSD_GROUPS + g + 1)].astype(f32)
            cb = cbs[k, g]
            for jp in range(pairs_per_group):
                j = g * pairs_per_group + jp
                cols = slice(LANES * j, LANES * (j + 1))
                rhs = jnp.concatenate([xs_ref[0, rows, cols], hprev_ref[0, k, :, cols],
                                       ht_ref[:, cols].astype(bf16)], axis=0)
                ys = []
                for par in range(2):
                    hf = 2 * j + par
                    hb = SSD_HEADS + hf
                    col_f = jnp.broadcast_to(acs[:, hf:hf + 1], (CHUNK, CHUNK))
                    col_b = jnp.broadcast_to(acs[:, hb:hb + 1], (CHUNK, CHUNK))
                    expo = jnp.where(past, col_f - rowt[hf:hf + 1, :],
                                     jnp.where(future, col_b - rowt[hb:hb + 1, :], lsum[hf:hf + 1, :]))
                    lhs = jnp.concatenate([(cb * jnp.exp2(expo)).astype(bf16),
                                           (cm_f * jnp.exp2(col_f)).astype(bf16),
                                           (cm_f * jnp.exp2(col_b)).astype(bf16)], axis=1)
                    ys.append(_dot(lhs, rhs))
                y_ref[:, cols] = jnp.where(low, ys[0], ys[1])
            _update_state(ht_ref, bc_ref, rows, g, xws[k, g], chunk_decay)

        y = y_ref[...] + dskip_ref[...] * xs_ref[0, rows, :].astype(f32)
        yz = y * _silu(z_ref[0, rows, :].astype(f32))
        ms = jnp.mean(yz * yz, axis=-1, keepdims=True)
        yn_ref[rows, :] = (yz * lax.rsqrt(ms + EPS) * gw_ref[...]).astype(bf16)

    gate = mod_ref[0][:, 2 * D_MODEL:]
    out_ref[0] = x1_ref[0] + gate * _dot(yn_ref[...], wout_ref[...])

    if final_state:
        @pl.when(step == pl.num_programs(1) - 1)
        def _fin():
            _write_state(hfin_ref, ht_ref)


def _ssd_states(xs, bc, acs, w, e_fwd, h0, *, latent):
    b, t, _ = xs.shape
    step_chunks = min(SSD_STEP_CHUNKS, t // CHUNK)
    rows = step_chunks * CHUNK
    blk = lambda bi, c: (bi, c, 0)
    per_b = lambda bi, c: (bi, 0, 0)
    in_specs = [pl.BlockSpec((1, rows, D_INNER), blk),
                pl.BlockSpec((1, rows, BC_WIDTH), blk),
                pl.BlockSpec((1, rows, LANES), blk),
                pl.BlockSpec((1, rows, LANES), blk),
                pl.BlockSpec((LANES, D_INNER), lambda bi, c: (0, 0))]
    args = [xs, bc, acs, w, e_fwd]
    if h0 is not None:
        in_specs.append(pl.BlockSpec((1, D_INNER, D_STATE), per_b))
        args.append(h0)
    out_specs = [pl.BlockSpec((1, step_chunks, D_STATE, D_INNER), lambda bi, c: (bi, c, 0, 0))]
    out_shape = [jax.ShapeDtypeStruct((b, t // CHUNK, D_STATE, D_INNER), bf16)]
    if not latent:
        out_specs.append(pl.BlockSpec((1, D_INNER, D_STATE), per_b))
        out_shape.append(jax.ShapeDtypeStruct((b, D_INNER, D_STATE), f32))
    return pl.pallas_call(
        functools.partial(_ssd_states_kernel, has_h0=h0 is not None, final_state=not latent),
        grid=(b, t // rows),
        in_specs=in_specs, out_specs=out_specs, out_shape=out_shape,
        scratch_shapes=[pltpu.VMEM((D_STATE, D_INNER), f32)],
        compiler_params=_params(2),
        name="ssd_states",
    )(*args)


def _ssd_main(xs, bc, acs, rowt, lsum, w, e_bwd, hprev, h0, z, x1, mod, gw, wout, dskip, *, latent):
    b, t, _ = xs.shape
    step_chunks = min(SSD_STEP_CHUNKS, t // CHUNK)
    rows = step_chunks * CHUNK
    n_steps = t // rows
    blk = lambda bi, c: (bi, n_steps - 1 - c, 0)
    blk4 = lambda bi, c: (bi, n_steps - 1 - c, 0, 0)
    const = lambda bi, c: (0, 0)
    per_b = lambda bi, c: (bi, 0, 0)
    row = (lambda bi, c: (bi + 1, 0, 0)) if latent else (lambda bi, c: (0, 0, 0))
    in_specs = [pl.BlockSpec((1, rows, D_INNER), blk),
                pl.BlockSpec((1, rows, BC_WIDTH), blk),
                pl.BlockSpec((1, rows, LANES), blk),
                pl.BlockSpec((1, step_chunks, LANES, CHUNK), blk4),
                pl.BlockSpec((1, step_chunks, LANES, CHUNK), blk4),
                pl.BlockSpec((1, rows, LANES), blk),
                pl.BlockSpec((LANES, D_INNER), const),
                pl.BlockSpec((1, step_chunks, D_STATE, D_INNER), blk4)]
    args = [xs, bc, acs, rowt, lsum, w, e_bwd, hprev]
    if h0 is not None:
        in_specs.append(pl.BlockSpec((1, D_INNER, D_STATE), per_b))
        args.append(h0)
    in_specs += [pl.BlockSpec((1, rows, D_INNER), blk),
                 pl.BlockSpec((1, rows, D_MODEL), blk),
                 pl.BlockSpec((1, 1, 3 * D_MODEL), row),
                 pl.BlockSpec((1, D_INNER), const),
                 pl.BlockSpec((D_INNER, D_MODEL), const),
                 pl.BlockSpec((1, D_INNER), const)]
    args += [z, x1, mod, gw, wout, dskip]
    out_specs = [pl.BlockSpec((1, rows, D_MODEL), blk)]
    out_shape = [jax.ShapeDtypeStruct((b, t, D_MODEL), f32)]
    if not latent:
        out_specs.append(pl.BlockSpec((1, D_INNER, D_STATE), per_b))
        out_shape.append(jax.ShapeDtypeStruct((b, D_INNER, D_STATE), f32))
    return pl.pallas_call(
        functools.partial(_ssd_main_kernel, has_h0=h0 is not None, final_state=not latent),
        grid=(b, n_steps),
        in_specs=in_specs, out_specs=out_specs, out_shape=out_shape,
        scratch_shapes=[pltpu.VMEM((D_STATE, D_INNER), f32), pltpu.VMEM((CHUNK, D_INNER), f32),
                        pltpu.VMEM((rows, D_INNER), bf16)],
        compiler_params=_params(2),
        name="ssd_main",
    )(*args)


def _rope_tables(n_tokens):
    rows = n_tokens // GRID_W
    row_ids = jnp.repeat(jnp.arange(rows), GRID_W).astype(f32)
    col_ids = jnp.tile(jnp.arange(GRID_W), rows).astype(f32)
    inv_freq = 1.0 / (ROPE_THETA ** (jnp.arange(0, AXIS_DIM, 2, dtype=f32) / AXIS_DIM))
    ang = jnp.stack([row_ids[:, None] * inv_freq, col_ids[:, None] * inv_freq], axis=1)
    cos, sin, zero = jnp.cos(ang), jnp.sin(ang), jnp.zeros_like(ang)
    head = lambda a, b_: jnp.stack([a, b_], axis=2).reshape(n_tokens, HEAD_DIM)
    two = lambda u: jnp.concatenate([u, u], axis=1)
    return two(head(cos, cos)), two(head(-sin, zero)), two(head(zero, sin))


def kernel(x_prompt, x_sample, cache_k_l0, cache_v_l0, state_fwd_l1, state_bwd_l1, c, c_ctx,
           l0_norm_w, l0_mod_w, l0_mod_b, l0_w_in, l0_q_norm, l0_k_norm, l0_w_out,
           l1_norm_w, l1_mod_w, l1_mod_b, l1_w_in, l1_conv_w, l1_conv_b, l1_dt_bias_f, l1_dt_bias_b,
           l1_a_log_f, l1_a_log_b, l1_d_skip, l1_gnorm_w, l1_w_out):
    dec_b = x_sample.shape[0]
    pad = lambda u, n: jnp.pad(u, (0, n - u.shape[0]))

    n_rows = 16
    cond = jnp.zeros((n_rows, D_MODEL), f32).at[0].set(c_ctx).at[1:1 + dec_b].set(c)
    mod0 = _ada_mod(cond, l0_mod_w, l0_mod_b).reshape(n_rows, 1, 3 * D_MODEL)
    mod1 = _ada_mod(cond, l1_mod_w, l1_mod_b).reshape(n_rows, 1, 3 * D_MODEL)

    q_scale = HEAD_DIM ** -0.5 * LOG2_E
    qn = (jnp.tile(l0_q_norm, N_HEADS) * q_scale).reshape(1, ATTN_WIDTH)
    kn = jnp.tile(l0_k_norm, N_KV_HEADS).reshape(1, KV_WIDTH)
    head_of = jnp.arange(ATTN_WIDTH) // HEAD_DIM
    e = (head_of[:, None] == jnp.arange(LANES)[None, :]).astype(bf16)
    et = e.T
    w_in0 = l0_w_in.astype(bf16)
    w_out0 = l0_w_out.astype(bf16)
    w_zx = l1_w_in.astype(bf16)
    w_dt = jnp.pad(l1_w_in[:, D_INNER + CONV_DIM:], ((0, 0), (0, LANES - 2 * SSD_HEADS))).astype(bf16)
    w_out1 = l1_w_out.astype(bf16)
    dt_bias = pad(jnp.concatenate([l1_dt_bias_f, l1_dt_bias_b]), LANES).reshape(1, LANES)
    alog = pad(jnp.concatenate([l1_a_log_f, l1_a_log_b]), LANES).reshape(1, LANES)
    dskip = jnp.repeat(l1_d_skip, SSD_HEAD_DIM).reshape(1, D_INNER)
    nw0 = l0_norm_w.reshape(1, D_MODEL)
    nw1 = l1_norm_w.reshape(1, D_MODEL)
    gw = l1_gnorm_w.reshape(1, D_INNER)
    conv_b = l1_conv_b.reshape(1, CONV_DIM)
    tri = jnp.arange(CHUNK)
    lmat = (tri[None, :] <= tri[:, None]).astype(bf16)
    umat = lmat.T
    rope_tabs = _rope_tables(x_sample.shape[1])
    ssd_head_of = jnp.arange(D_INNER) // SSD_HEAD_DIM
    e_fwd = (jnp.arange(LANES)[:, None] == ssd_head_of[None, :]).astype(bf16)
    e_bwd = (jnp.arange(LANES)[:, None] == SSD_HEADS + ssd_head_of[None, :]).astype(bf16)

    kc = cache_k_l0.transpose(0, 2, 1, 3).astype(bf16)
    vc = cache_v_l0.transpose(0, 2, 3, 1).astype(bf16)
    kk_ctx = jnp.concatenate([kc, kc], axis=-1)
    vt_ctx = jnp.concatenate(
        [vc, jnp.ones(vc.shape[:2] + (VT_ROWS - HEAD_DIM, vc.shape[3]), bf16)], axis=2)
    h0_f = state_fwd_l1.reshape(dec_b, D_INNER, D_STATE)
    h0_b = state_bwd_l1.reshape(dec_b, D_INNER, D_STATE)

    outs = {}
    for latent, x in ((False, x_prompt), (True, x_sample)):
        res = _l0_inproj(x, mod0, nw0, w_in0, qn, kn, e, et, rope_tabs, latent=latent,
                         tm=4 * L0_SUB_ROWS if latent else L0_SUB_ROWS)
        if latent:
            q, kk, vt, gs = res
            kv = [kk_ctx, vt_ctx, kk, vt]
        else:
            q, kk, vt, gs, k_new, v_new = res
            kv = [kk, vt]
        o = _attention(q, kv, tq=4 * Q_SUB_ROWS if latent else Q_SUB_ROWS)
        x1, z, xs, bc, acs, rowt, w, lsum = _mid(
            x, o, gs, mod0, mod1, w_out0, nw1, w_zx, w_dt, l1_conv_w, conv_b, dt_bias, alog,
            lmat, umat, latent=latent, tm=512 if latent else 256)
        fwd = _ssd_states(xs, bc, acs, w, e_fwd, h0_f if latent else None, latent=latent)
        bwd = _ssd_main(xs, bc, acs, rowt, lsum, w, e_bwd, fwd[0], h0_b if latent else None,
                        z, x1, mod1, gw, w_out1, dskip, latent=latent)
        if latent:
            outs["y_sample"] = bwd[0]
        else:
            b = x.shape[0]
            outs["y_prompt"] = bwd[0]
            outs["k"] = k_new.reshape(b, -1, N_KV_HEADS, HEAD_DIM)
            outs["v"] = v_new.reshape(b, -1, N_KV_HEADS, HEAD_DIM)
            outs["hf"] = fwd[1].reshape(b, SSD_HEADS, SSD_HEAD_DIM, D_STATE)
            outs["hb"] = bwd[1].reshape(b, SSD_HEADS, SSD_HEAD_DIM, D_STATE)
    return (outs["y_prompt"], outs["y_sample"], outs["k"], outs["v"], outs["hf"], outs["hb"])
```

```python
import functools
import math

import jax
import jax.numpy as jnp
from jax import lax
from jax.experimental import pallas as pl
from jax.experimental.pallas import tpu as pltpu

f32 = jnp.float32
bf16 = jnp.bfloat16

D_MODEL = 1024
GRID_W = 64
EPS = 1e-6
N_HEADS = 16
N_KV_HEADS = 4
HEAD_DIM = 64
ATTN_WIDTH = N_HEADS * HEAD_DIM
KV_WIDTH = N_KV_HEADS * HEAD_DIM
AXIS_DIM = HEAD_DIM // 2
ROPE_THETA = 10000.0
D_INNER = 2048
SSD_HEAD_DIM = 64
SSD_HEADS = D_INNER // SSD_HEAD_DIM
SSD_GROUPS = 4
HEADS_PER_GROUP = SSD_HEADS // SSD_GROUPS
D_STATE = 128
CHUNK = 128
BC_WIDTH = 2 * SSD_GROUPS * D_STATE
CONV_DIM = D_INNER + BC_WIDTH

LANES = 128
BF16_ROWS = 16
HALO = BF16_ROWS
PROJ_BLOCK = 256
PROJ_LOOKAHEAD = 2
L0_SUB_ROWS = 256
SSD_STEP_CHUNKS = 4
HALF_VREG_ROWS = 8
GROUP_W = HEADS_PER_GROUP * SSD_HEAD_DIM
VT_ROWS = HEAD_DIM + BF16_ROWS
Q_SUB_ROWS = 256
KEY_CHUNK = 256
SCORE_LOOKAHEAD = 8
VMEM_LIMIT = 56 * 1024 * 1024

LOG2_E = math.log2(math.e)


def _params(n_axes):
    return pltpu.CompilerParams(
        dimension_semantics=("arbitrary",) * n_axes, vmem_limit_bytes=VMEM_LIMIT)


def _dot(a, b):
    return jnp.dot(a, b, preferred_element_type=f32)


def _dot_nt(a, b):
    return lax.dot_general(a, b, (((1,), (1,)), ((), ())), preferred_element_type=f32)


def _split(x):
    hi = x.astype(bf16)
    lo = (x - hi.astype(f32)).astype(bf16)
    return hi, lo


def _silu(x):
    return x * jax.nn.sigmoid(x)


def _softplus(x):
    return jnp.maximum(x, 0.0) + jnp.log1p(jnp.exp(-jnp.abs(x)))


def _modulated_norm(x, norm_w, shift, scale):
    ms = jnp.mean(x * x, axis=-1, keepdims=True)
    return (x * lax.rsqrt(ms + EPS) * norm_w) * (1.0 + scale) + shift


def _ada_mod_kernel(cond_ref, w_ref, b_ref, o_ref):
    s = _silu(cond_ref[...])
    sh, sl = _split(s)
    wh, wl = _split(w_ref[...])
    o_ref[...] = _dot(sh, wh) + (_dot(sh, wl) + _dot(sl, wh)) + b_ref[...]


def _ada_mod(cond, mod_w, mod_b):
    rows = cond.shape[0]
    tn = 1024
    return pl.pallas_call(
        _ada_mod_kernel,
        grid=(3 * D_MODEL // tn,),
        in_specs=[pl.BlockSpec((rows, D_MODEL), lambda j: (0, 0)),
                  pl.BlockSpec((D_MODEL, tn), lambda j: (0, j)),
                  pl.BlockSpec((1, tn), lambda j: (0, j))],
        out_specs=pl.BlockSpec((rows, tn), lambda j: (0, j)),
        out_shape=jax.ShapeDtypeStruct((rows, 3 * D_MODEL), f32),
        compiler_params=_params(1),
        name="ada_mod",
    )(cond, mod_w, mod_b.reshape(1, -1))


def _head_rms(t, e, et, w):
    hi, lo = _split(t * t)
    ss = _dot(hi, e) + _dot(lo, e)
    rh, rl = _split(lax.rsqrt(ss * (1.0 / HEAD_DIM) + EPS))
    return t * (_dot(rh, et) + _dot(rl, et)) * w


def _rope(t, cos, sin_a, sin_b):
    width = t.shape[1]
    rep = width // LANES
    tile = lambda u: jnp.concatenate([u] * rep, axis=1)
    return (t * tile(cos) + pltpu.roll(t, width - AXIS_DIM // 2, 1) * tile(sin_a)
            + pltpu.roll(t, AXIS_DIM // 2, 1) * tile(sin_b))


def _l0_inproj_kernel(*refs, latent):
    x_ref, mod_ref, nw_ref, w_ref, qn_ref, kn_ref, e_ref, et_ref = refs[:8]
    refs = refs[8:]
    if latent:
        cos_ref, sa_ref, sb_ref = refs[:3]
        q_ref, kk_ref, vt_ref, gs_ref = refs[3:]
    else:
        q_ref, kk_ref, vt_ref, gs_ref, knew_ref, vnew_ref = refs
    mod = mod_ref[0]
    e = e_ref[...]
    et = et_ref[...]
    subs = [slice(r0, r0 + L0_SUB_ROWS) for r0 in range(0, x_ref.shape[1], L0_SUB_ROWS)]
    projs = []
    for rows in subs:
        h = _modulated_norm(x_ref[0, rows, :], nw_ref[...], mod[:, :D_MODEL], mod[:, D_MODEL:2 * D_MODEL])
        projs.append(_dot(h.astype(bf16), w_ref[...]))
    low = lax.broadcasted_iota(jnp.int32, (L0_SUB_ROWS, LANES), 1) < HEAD_DIM
    ones = jnp.ones((VT_ROWS - HEAD_DIM, L0_SUB_ROWS), bf16)
    for rows, proj in zip(subs, projs):
        q = proj[:, :ATTN_WIDTH]
        k = proj[:, ATTN_WIDTH:ATTN_WIDTH + KV_WIDTH]
        v = proj[:, ATTN_WIDTH + KV_WIDTH:ATTN_WIDTH + 2 * KV_WIDTH]
        g = proj[:, ATTN_WIDTH + 2 * KV_WIDTH:]
        q = _head_rms(q, e, et, qn_ref[...])
        k = _head_rms(k, e[:KV_WIDTH], et[:, :KV_WIDTH], kn_ref[...])
        if latent:
            cos, sa, sb = cos_ref[rows, :], sa_ref[rows, :], sb_ref[rows, :]
            q = _rope(q, cos, sa, sb)
            k = _rope(k, cos, sa, sb)
        else:
            knew_ref[0, rows, :] = k
            vnew_ref[0, rows, :] = v
        q_ref[0, rows, :] = q.astype(bf16)
        gs_ref[0, rows, :] = _silu(g).astype(bf16)
        v_t = v.T
        for j in range(KV_WIDTH // LANES):
            kb = k[:, LANES * j:LANES * (j + 1)]
            kbs = pltpu.roll(kb, HEAD_DIM, 1)
            kk_ref[0, 2 * j, rows, :] = jnp.where(low, kb, kbs).astype(bf16)
            kk_ref[0, 2 * j + 1, rows, :] = jnp.where(low, kbs, kb).astype(bf16)
        for kvh in range(N_KV_HEADS):
            vt_ref[0, kvh, :HEAD_DIM, rows] = v_t[HEAD_DIM * kvh:HEAD_DIM * (kvh + 1)].astype(bf16)
            vt_ref[0, kvh, HEAD_DIM:, rows] = ones


def _l0_inproj(x, mod, norm_w, w_in, qn, kn, e, et, rope_tabs, *, latent, tm):
    b, t, _ = x.shape
    attn_in = w_in.shape[1]
    row = (lambda bi, i: (bi + 1, 0, 0)) if latent else (lambda bi, i: (0, 0, 0))
    const = lambda bi, i: (0, 0)
    in_specs = [pl.BlockSpec((1, tm, D_MODEL), lambda bi, i: (bi, i, 0)),
                pl.BlockSpec((1, 1, 3 * D_MODEL), row),
                pl.BlockSpec((1, D_MODEL), const),
                pl.BlockSpec((D_MODEL, attn_in), const),
                pl.BlockSpec((1, ATTN_WIDTH), const),
                pl.BlockSpec((1, KV_WIDTH), const),
                pl.BlockSpec((ATTN_WIDTH, LANES), const),
                pl.BlockSpec((LANES, ATTN_WIDTH), const)]
    args = [x, mod, norm_w, w_in, qn, kn, e, et]
    tok = lambda bi, i: (bi, i, 0)
    out_specs = [pl.BlockSpec((1, tm, ATTN_WIDTH), tok),
                 pl.BlockSpec((1, N_KV_HEADS, tm, LANES), lambda bi, i: (bi, 0, i, 0)),
                 pl.BlockSpec((1, N_KV_HEADS, VT_ROWS, tm), lambda bi, i: (bi, 0, 0, i)),
                 pl.BlockSpec((1, tm, ATTN_WIDTH), tok)]
    out_shape = [jax.ShapeDtypeStruct((b, t, ATTN_WIDTH), bf16),
                 jax.ShapeDtypeStruct((b, N_KV_HEADS, t, LANES), bf16),
                 jax.ShapeDtypeStruct((b, N_KV_HEADS, VT_ROWS, t), bf16),
                 jax.ShapeDtypeStruct((b, t, ATTN_WIDTH), bf16)]
    if latent:
        in_specs += [pl.BlockSpec((tm, LANES), lambda bi, i: (i, 0))] * 3
        args += list(rope_tabs)
    else:
        out_specs += [pl.BlockSpec((1, tm, KV_WIDTH), tok)] * 2
        out_shape += [jax.ShapeDtypeStruct((b, t, KV_WIDTH), f32)] * 2
    return pl.pallas_call(
        functools.partial(_l0_inproj_kernel, latent=latent),
        grid=(b, t // tm),
        in_specs=in_specs, out_specs=out_specs, out_shape=out_shape,
        compiler_params=_params(2),
        name="l0_inproj_latent" if latent else "l0_inproj_context",
    )(*args)


def _attn_kernel(*refs, n_src):
    q_ref = refs[0]
    kv_refs = refs[1:1 + 2 * n_src]
    o_ref = refs[-1]
    n_sub = q_ref.shape[1] // Q_SUB_ROWS
    low = lax.broadcasted_iota(jnp.int32, (Q_SUB_ROWS, LANES), 1) < HEAD_DIM
    zero = jnp.zeros((Q_SUB_ROWS, LANES), bf16)
    chunks = []
    for i in range(n_src):
        n_keys = kv_refs[2 * i].shape[2]
        width = min(KEY_CHUNK, n_keys)
        chunks += [(kv_refs[2 * i], kv_refs[2 * i + 1], slice(c0, c0 + width))
                   for c0 in range(0, n_keys, width)]
    steps = [(sub, head, ch) for sub in range(n_sub) for head in range(4)
             for ch in range(len(chunks))]

    def q_masked(sub, head):
        qp = q_ref[0, Q_SUB_ROWS * sub:Q_SUB_ROWS * (sub + 1),
                   LANES * (head // 2):LANES * (head // 2 + 1)]
        return jnp.where(low, qp, zero) if head % 2 == 0 else jnp.where(low, zero, qp)

    def scores(step):
        sub, head, ch = step
        k_ref, _, keys = chunks[ch]
        return _dot_nt(k_ref[0, 0, keys, :], q_masked(sub, head))

    halves = []
    m = acc = None
    queue = [scores(st) for st in steps[:SCORE_LOOKAHEAD]]
    for idx, (sub, head, ch) in enumerate(steps):
        s = queue.pop(0)
        if idx + SCORE_LOOKAHEAD < len(steps):
            queue.append(scores(steps[idx + SCORE_LOOKAHEAD]))
        _, vt_ref, keys = chunks[ch]
        s = s.astype(bf16)
        slab = s.shape[0] // 4
        m_c = jnp.maximum(jnp.maximum(s[:slab], s[slab:2 * slab]),
                          jnp.maximum(s[2 * slab:3 * slab], s[3 * slab:]))
        m_c = jnp.max(m_c.astype(f32), axis=0, keepdims=True)
        m_new = m_c if ch == 0 else jnp.maximum(m, m_c)
        pv = _dot(vt_ref[0, 0, :, keys], jnp.exp2(s - m_new.astype(bf16)))
        acc = pv if ch == 0 else acc * jnp.exp2(m - m_new) + pv
        m = m_new
        if ch == len(chunks) - 1:
            halves.append(acc[:HEAD_DIM] / acc[HEAD_DIM:HEAD_DIM + 1])
            if head % 2 == 1:
                o_ref[0, Q_SUB_ROWS * sub:Q_SUB_ROWS * (sub + 1),
                      LANES * (head // 2):LANES * (head // 2 + 1)] = (
                    jnp.concatenate(halves, axis=0).T.astype(bf16))
                halves = []


def _attention(q, kv_sources, *, tq):
    b, t, _ = q.shape
    in_specs = [pl.BlockSpec((1, tq, 2 * LANES), lambda bi, g, i: (bi, i, g))]
    args = [q]
    for arr in kv_sources:
        in_specs.append(pl.BlockSpec((1, 1) + arr.shape[2:], lambda bi, g, i: (bi, g, 0, 0)))
        args.append(arr)
    return pl.pallas_call(
        functools.partial(_attn_kernel, n_src=len(kv_sources) // 2),
        grid=(b, N_KV_HEADS, t // tq),
        in_specs=in_specs,
        out_specs=pl.BlockSpec((1, tq, 2 * LANES), lambda bi, g, i: (bi, i, g)),
        out_shape=jax.ShapeDtypeStruct((b, t, ATTN_WIDTH), bf16),
        compiler_params=_params(3),
        name="attention_latent" if len(kv_sources) > 2 else "attention_context",
    )(*args)


def _mid_kernel(x_ref, xp_ref, xn_ref, o_ref, op_ref, on_ref, g_ref, gp_ref, gn_ref,
                mod0_ref, mod1_ref, wout_ref, nw_ref, wzx_ref, wdt_ref, cw_ref, cb_ref, dtb_ref,
                alog_ref, lmat_ref, umat_ref,
                x1_ref, z_ref, xs_ref, bc_ref, acs_ref, rowt_ref, w_ref, lsum_ref):
    i = pl.program_id(1)
    last = pl.num_programs(1) - 1
    tm = x_ref.shape[1]
    rows = tm + 2 * HALO
    body = slice(HALO, HALO + tm)
    cat = lambda a, b_, c: jnp.concatenate([a[0], b_[0], c[0]], axis=0)
    gated = cat(op_ref, o_ref, on_ref) * cat(gp_ref, g_ref, gn_ref)
    x1 = cat(xp_ref, x_ref, xn_ref) + mod0_ref[0][:, 2 * D_MODEL:] * _dot(gated, wout_ref[...])
    x1_ref[0] = x1[body]
    mod = mod1_ref[0]
    h = _modulated_norm(x1, nw_ref[...], mod[:, :D_MODEL], mod[:, D_MODEL:2 * D_MODEL])
    h_body = h[body].astype(bf16)
    h = jnp.concatenate([jnp.where(i == 0, 0.0, h[:HALO]).astype(bf16), h_body,
                         jnp.where(i == last, 0.0, h[HALO + tm:]).astype(bf16)], axis=0)

    n_z = D_INNER // PROJ_BLOCK
    tasks = []
    for j in range(CONV_DIM // PROJ_BLOCK):
        tasks.append(("conv", j))
        if j < n_z:
            tasks.append(("z", j))

    def project(task):
        kind, j = task
        if kind == "z":
            return _dot(h_body, wzx_ref[:, PROJ_BLOCK * j:PROJ_BLOCK * (j + 1)])
        w_blk = wzx_ref[:, D_INNER + PROJ_BLOCK * j:D_INNER + PROJ_BLOCK * (j + 1)]
        return jnp.concatenate([_dot(h[:rows // 2], w_blk), _dot(h[rows // 2:], w_blk)], axis=0)

    queue = [project(t) for t in tasks[:PROJ_LOOKAHEAD]]
    for idx, (kind, j) in enumerate(tasks):
        blk = queue.pop(0)
        if idx + PROJ_LOOKAHEAD < len(tasks):
            queue.append(project(tasks[idx + PROJ_LOOKAHEAD]))
        cols = slice(PROJ_BLOCK * j, PROJ_BLOCK * (j + 1))
        if kind == "z":
            z_ref[0, :, cols] = blk.astype(bf16)
            continue
        act = _silu(cw_ref[0:1, cols] * pltpu.roll(blk, 1, 0)[body] + cw_ref[1:2, cols] * blk[body]
                    + cw_ref[2:3, cols] * pltpu.roll(blk, rows - 1, 0)[body]
                    + cb_ref[:, cols]).astype(bf16)
        if j < n_z:
            xs_ref[0, :, cols] = act
        else:
            bc_ref[0, :, PROJ_BLOCK * (j - n_z):PROJ_BLOCK * (j - n_z + 1)] = act

    dt = _softplus(_dot(h_body, wdt_ref[...]) + dtb_ref[...])
    a = -jnp.exp(alog_ref[...])
    fwd_lane = lax.broadcasted_iota(jnp.int32, (CHUNK, LANES), 1) < SSD_HEADS
    lower, upper = lmat_ref[...], umat_ref[...]
    for c in range(tm // CHUNK):
        dtc = dt[CHUNK * c:CHUNK * (c + 1)]
        ah, al = _split(dtc * a)
        acs = jnp.where(fwd_lane, _dot(lower, ah) + _dot(lower, al), _dot(upper, ah) + _dot(upper, al))
        end_row = jnp.where(fwd_lane[0:1], acs[CHUNK - 1:CHUNK], acs[0:1])
        w_ref[0, CHUNK * c:CHUNK * (c + 1)] = dtc * jnp.exp(end_row - acs)
        acs2 = acs * LOG2_E
        acs_ref[0, CHUNK * c:CHUNK * (c + 1)] = acs2
        rowt_ref[0, c] = (acs2 - jnp.log2(dtc)).T
        lsum_ref[0, c] = jnp.log2(dtc + pltpu.roll(dtc, LANES - SSD_HEADS, 1)).T


def _mid(x, o, gs, mod0, mod1, w_out0, norm_w, w_zx, w_dt, conv_w, conv_b, dt_bias, alog, lmat, umat,
         *, latent, tm):
    b, t, _ = x.shape
    row = (lambda bi, i: (bi + 1, 0, 0)) if latent else (lambda bi, i: (0, 0, 0))
    const = lambda bi, i: (0, 0)
    tok = lambda bi, i: (bi, i, 0)
    hb = tm // HALO
    n_hb = t // HALO
    prev_map = lambda bi, i: (bi, jnp.maximum(i * hb - 1, 0), 0)
    next_map = lambda bi, i: (bi, jnp.minimum((i + 1) * hb, n_hb - 1), 0)
    halo3 = lambda width: [pl.BlockSpec((1, tm, width), tok),
                           pl.BlockSpec((1, HALO, width), prev_map),
                           pl.BlockSpec((1, HALO, width), next_map)]
    resident = lambda shape: pl.BlockSpec(shape, const, pipeline_mode=pl.Buffered(1))
    chunk4 = lambda bi, i: (bi, i, 0, 0)
    ncs = tm // CHUNK
    return pl.pallas_call(
        _mid_kernel,
        grid=(b, t // tm),
        in_specs=halo3(D_MODEL) + halo3(ATTN_WIDTH) + halo3(ATTN_WIDTH) + [
            pl.BlockSpec((1, 1, 3 * D_MODEL), row),
            pl.BlockSpec((1, 1, 3 * D_MODEL), row),
            resident((ATTN_WIDTH, D_MODEL)),
            pl.BlockSpec((1, D_MODEL), const),
            resident((D_MODEL, D_INNER + CONV_DIM)),
            resident((D_MODEL, LANES)),
            pl.BlockSpec((3, CONV_DIM), const),
            pl.BlockSpec((1, CONV_DIM), const),
            pl.BlockSpec((1, LANES), const),
            pl.BlockSpec((1, LANES), const),
            pl.BlockSpec((CHUNK, CHUNK), const),
            pl.BlockSpec((CHUNK, CHUNK), const)],
        out_specs=[pl.BlockSpec((1, tm, D_MODEL), tok),
                   pl.BlockSpec((1, tm, D_INNER), tok),
                   pl.BlockSpec((1, tm, D_INNER), tok),
                   pl.BlockSpec((1, tm, BC_WIDTH), tok),
                   pl.BlockSpec((1, tm, LANES), tok),
                   pl.BlockSpec((1, ncs, LANES, CHUNK), chunk4),
                   pl.BlockSpec((1, tm, LANES), tok),
                   pl.BlockSpec((1, ncs, LANES, CHUNK), chunk4)],
        out_shape=[jax.ShapeDtypeStruct((b, t, D_MODEL), f32),
                   jax.ShapeDtypeStruct((b, t, D_INNER), bf16),
                   jax.ShapeDtypeStruct((b, t, D_INNER), bf16),
                   jax.ShapeDtypeStruct((b, t, BC_WIDTH), bf16),
                   jax.ShapeDtypeStruct((b, t, LANES), f32),
                   jax.ShapeDtypeStruct((b, t // CHUNK, LANES, CHUNK), f32),
                   jax.ShapeDtypeStruct((b, t, LANES), f32),
                   jax.ShapeDtypeStruct((b, t // CHUNK, LANES, CHUNK), f32)],
        compiler_params=_params(2),
        name="mid_latent" if latent else "mid_context",
    )(x, x, x, o, o, o, gs, gs, gs, mod0, mod1, w_out0, norm_w, w_zx, w_dt, conv_w, conv_b,
      dt_bias, alog, lmat, umat)


def _init_state(ht_ref, h0_ref):
    if h0_ref is None:
        ht_ref[...] = jnp.zeros_like(ht_ref)
    else:
        for j in range(SSD_HEADS // 2):
            ht_ref[:, LANES * j:LANES * (j + 1)] = h0_ref[0, LANES * j:LANES * (j + 1), :].T


def _write_state(hfin_ref, ht_ref):
    for j in range(SSD_HEADS // 2):
        hfin_ref[0, LANES * j:LANES * (j + 1), :] = ht_ref[:, LANES * j:LANES * (j + 1)].T


def _lane_expand_decay(acs_row, e_ref):
    hi, lo = _split(jnp.broadcast_to(jnp.exp2(acs_row), (HALF_VREG_ROWS, LANES)))
    return (_dot(hi, e_ref[...]) + _dot(lo, e_ref[...]))[0:1]


def _weighted_x(xs_ref, w_ref, e_ref, rows, g):
    gcols = slice(GROUP_W * g, GROUP_W * (g + 1))
    return xs_ref[0, rows, gcols] * _dot(w_ref[0, rows, :].astype(bf16), e_ref[:, gcols]).astype(bf16)


def _update_state(ht_ref, bc_ref, rows, g, xw, chunk_decay):
    gcols = slice(GROUP_W * g, GROUP_W * (g + 1))
    bm = bc_ref[0, rows, D_STATE * g:D_STATE * (g + 1)]
    st = _dot(bm.astype(f32).T.astype(bf16), xw)
    ht_ref[:, gcols] = ht_ref[:, gcols] * chunk_decay[:, gcols] + st


def _ssd_states_kernel(*refs, has_h0, final_state):
    xs_ref, bc_ref, acs_ref, w_ref, e_ref = refs[:5]
    refs = refs[5:]
    h0_ref = None
    if has_h0:
        h0_ref, refs = refs[0], refs[1:]
    hprev_ref, refs = refs[0], refs[1:]
    if final_state:
        hfin_ref, refs = refs[0], refs[1:]
    ht_ref = refs[0]
    step = pl.program_id(1)

    @pl.when(step == 0)
    def _init():
        _init_state(ht_ref, h0_ref)

    n_chunks = xs_ref.shape[1] // CHUNK
    xws = {(k, g): _weighted_x(xs_ref, w_ref, e_ref, slice(CHUNK * k, CHUNK * (k + 1)), g)
           for k in range(n_chunks) for g in range(SSD_GROUPS)}
    for k in range(n_chunks):
        rows = slice(CHUNK * k, CHUNK * (k + 1))
        chunk_decay = _lane_expand_decay(acs_ref[0, CHUNK * (k + 1) - 1:CHUNK * (k + 1), :], e_ref)
        hprev_ref[0, k] = ht_ref[...].astype(bf16)
        for g in range(SSD_GROUPS):
            _update_state(ht_ref, bc_ref, rows, g, xws[k, g], chunk_decay)

    if final_state:
        @pl.when(step == pl.num_programs(1) - 1)
        def _fin():
            _write_state(hfin_ref, ht_ref)


def _ssd_main_kernel(*refs, has_h0, final_state):
    xs_ref, bc_ref, acs_ref, rowt_ref, lsum_ref, w_ref, e_ref, hprev_ref = refs[:8]
    refs = refs[8:]
    h0_ref = None
    if has_h0:
        h0_ref, refs = refs[0], refs[1:]
    z_ref, x1_ref, mod_ref, gw_ref, wout_ref, dskip_ref = refs[:6]
    out_ref, refs = refs[6], refs[7:]
    if final_state:
        hfin_ref, refs = refs[0], refs[1:]
    ht_ref, y_ref, yn_ref = refs
    step = pl.program_id(1)

    @pl.when(step == 0)
    def _init():
        _init_state(ht_ref, h0_ref)

    pairs_per_group = HEADS_PER_GROUP // 2
    n_chunks = xs_ref.shape[1] // CHUNK
    ri = lax.broadcasted_iota(jnp.int32, (CHUNK, CHUNK), 0)
    ci = lax.broadcasted_iota(jnp.int32, (CHUNK, CHUNK), 1)
    past, future = ri > ci, ri < ci
    low = lax.broadcasted_iota(jnp.int32, (CHUNK, LANES), 1) < SSD_HEAD_DIM
    chunk_order = list(range(n_chunks - 1, -1, -1))

    cbs, xws = {}, {}
    for k in chunk_order:
        rows = slice(CHUNK * k, CHUNK * (k + 1))
        for g in range(SSD_GROUPS):
            bm = bc_ref[0, rows, D_STATE * g:D_STATE * (g + 1)]
            cm = bc_ref[0, rows, D_STATE * (SSD_GROUPS + g):D_STATE * (SSD_GROUPS + g + 1)]
            cbs[k, g] = _dot_nt(cm, bm)
            xws[k, g] = _weighted_x(xs_ref, w_ref, e_ref, rows, g)

    for k in chunk_order:
        rows = slice(CHUNK * k, CHUNK * (k + 1))
        acs = acs_ref[0, rows, :]
        rowt = rowt_ref[0, k]
        lsum = lsum_ref[0, k]
        chunk_decay = _lane_expand_decay(acs[0:1, :], e_ref)
        for g in range(SSD_GROUPS):
            cm_f = bc_ref[0, rows, D_STATE * (SSD_GROUPS + g):D_STATE * (SSD_GROUPS + g + 1)].astype(f32)
            cb = cbs[k, g]
            for jp in range(pairs_per_group):
                j = g * pairs_per_group + jp
                cols = slice(LANES * j, LANES * (j + 1))
                rhs = jnp.concatenate([xs_ref[0, rows, cols], hprev_ref[0, k, :, cols],
                                       ht_ref[:, cols].astype(bf16)], axis=0)
                ys = []
                for par in range(2):
                    hf = 2 * j + par
                    hb = SSD_HEADS + hf
                    col_f = jnp.broadcast_to(acs[:, hf:hf + 1], (CHUNK, CHUNK))
                    col_b = jnp.broadcast_to(acs[:, hb:hb + 1], (CHUNK, CHUNK))
                    expo = jnp.where(past, col_f - rowt[hf:hf + 1, :],
                                     jnp.where(future, col_b - rowt[hb:hb + 1, :], lsum[hf:hf + 1, :]))
                    lhs = jnp.concatenate([(cb * jnp.exp2(expo)).astype(bf16),
                                           (cm_f * jnp.exp2(col_f)).astype(bf16),
                                           (cm_f * jnp.exp2(col_b)).astype(bf16)], axis=1)
                    ys.append(_dot(lhs, rhs))
                y_ref[:, cols] = jnp.where(low, ys[0], ys[1])
            _update_state(ht_ref, bc_ref, rows, g, xws[k, g], chunk_decay)

        y = y_ref[...] + dskip_ref[...] * xs_ref[0, rows, :].astype(f32)
        yz = y * _silu(z_ref[0, rows, :].astype(f32))
        ms = jnp.mean(yz * yz, axis=-1, keepdims=True)
        yn_ref[rows, :] = (yz * lax.rsqrt(ms + EPS) * gw_ref[...]).astype(bf16)

    gate = mod_ref[0][:, 2 * D_MODEL:]
    out_ref[0] = x1_ref[0] + gate * _dot(yn_ref[...], wout_ref[...])

    if final_state:
        @pl.when(step == pl.num_programs(1) - 1)
        def _fin():
            _write_state(hfin_ref, ht_ref)


def _ssd_states(xs, bc, acs, w, e_fwd, h0, *, latent):
    b, t, _ = xs.shape
    step_chunks = min(SSD_STEP_CHUNKS, t // CHUNK)
    rows = step_chunks * CHUNK
    blk = lambda bi, c: (bi, c, 0)
    per_b = lambda bi, c: (bi, 0, 0)
    in_specs = [pl.BlockSpec((1, rows, D_INNER), blk),
                pl.BlockSpec((1, rows, BC_WIDTH), blk),
                pl.BlockSpec((1, rows, LANES), blk),
                pl.BlockSpec((1, rows, LANES), blk),
                pl.BlockSpec((LANES, D_INNER), lambda bi, c: (0, 0))]
    args = [xs, bc, acs, w, e_fwd]
    if h0 is not None:
        in_specs.append(pl.BlockSpec((1, D_INNER, D_STATE), per_b))
        args.append(h0)
    out_specs = [pl.BlockSpec((1, step_chunks, D_STATE, D_INNER), lambda bi, c: (bi, c, 0, 0))]
    out_shape = [jax.ShapeDtypeStruct((b, t // CHUNK, D_STATE, D_INNER), bf16)]
    if not latent:
        out_specs.append(pl.BlockSpec((1, D_INNER, D_STATE), per_b))
        out_shape.append(jax.ShapeDtypeStruct((b, D_INNER, D_STATE), f32))
    return pl.pallas_call(
        functools.partial(_ssd_states_kernel, has_h0=h0 is not None, final_state=not latent),
        grid=(b, t // rows),
        in_specs=in_specs, out_specs=out_specs, out_shape=out_shape,
        scratch_shapes=[pltpu.VMEM((D_STATE, D_INNER), f32)],
        compiler_params=_params(2),
        name="ssd_states",
    )(*args)


def _ssd_main(xs, bc, acs, rowt, lsum, w, e_bwd, hprev, h0, z, x1, mod, gw, wout, dskip, *, latent):
    b, t, _ = xs.shape
    step_chunks = min(SSD_STEP_CHUNKS, t // CHUNK)
    rows = step_chunks * CHUNK
    n_steps = t // rows
    blk = lambda bi, c: (bi, n_steps - 1 - c, 0)
    blk4 = lambda bi, c: (bi, n_steps - 1 - c, 0, 0)
    const = lambda bi, c: (0, 0)
    per_b = lambda bi, c: (bi, 0, 0)
    row = (lambda bi, c: (bi + 1, 0, 0)) if latent else (lambda bi, c: (0, 0, 0))
    in_specs = [pl.BlockSpec((1, rows, D_INNER), blk),
                pl.BlockSpec((1, rows, BC_WIDTH), blk),
                pl.BlockSpec((1, rows, LANES), blk),
                pl.BlockSpec((1, step_chunks, LANES, CHUNK), blk4),
                pl.BlockSpec((1, step_chunks, LANES, CHUNK), blk4),
                pl.BlockSpec((1, rows, LANES), blk),
                pl.BlockSpec((LANES, D_INNER), const),
                pl.BlockSpec((1, step_chunks, D_STATE, D_INNER), blk4)]
    args = [xs, bc, acs, rowt, lsum, w, e_bwd, hprev]
    if h0 is not None:
        in_specs.append(pl.BlockSpec((1, D_INNER, D_STATE), per_b))
        args.append(h0)
    in_specs += [pl.BlockSpec((1, rows, D_INNER), blk),
                 pl.BlockSpec((1, rows, D_MODEL), blk),
                 pl.BlockSpec((1, 1, 3 * D_MODEL), row),
                 pl.BlockSpec((1, D_INNER), const),
                 pl.BlockSpec((D_INNER, D_MODEL), const),
                 pl.BlockSpec((1, D_INNER), const)]
    args += [z, x1, mod, gw, wout, dskip]
    out_specs = [pl.BlockSpec((1, rows, D_MODEL), blk)]
    out_shape = [jax.ShapeDtypeStruct((b, t, D_MODEL), f32)]
    if not latent:
        out_specs.append(pl.BlockSpec((1, D_INNER, D_STATE), per_b))
        out_shape.append(jax.ShapeDtypeStruct((b, D_INNER, D_STATE), f32))
    return pl.pallas_call(
        functools.partial(_ssd_main_kernel, has_h0=h0 is not None, final_state=not latent),
        grid=(b, n_steps),
        in_specs=in_specs, out_specs=out_specs, out_shape=out_shape,
        scratch_shapes=[pltpu.VMEM((D_STATE, D_INNER), f32), pltpu.VMEM((CHUNK, D_INNER), f32),
                        pltpu.VMEM((rows, D_INNER), bf16)],
        compiler_params=_params(2),
        name="ssd_main",
    )(*args)


def _rope_tables(n_tokens):
    rows = n_tokens // GRID_W
    row_ids = jnp.repeat(jnp.arange(rows), GRID_W).astype(f32)
    col_ids = jnp.tile(jnp.arange(GRID_W), rows).astype(f32)
    inv_freq = 1.0 / (ROPE_THETA ** (jnp.arange(0, AXIS_DIM, 2, dtype=f32) / AXIS_DIM))
    ang = jnp.stack([row_ids[:, None] * inv_freq, col_ids[:, None] * inv_freq], axis=1)
    cos, sin, zero = jnp.cos(ang), jnp.sin(ang), jnp.zeros_like(ang)
    head = lambda a, b_: jnp.stack([a, b_], axis=2).reshape(n_tokens, HEAD_DIM)
    two = lambda u: jnp.concatenate([u, u], axis=1)
    return two(head(cos, cos)), two(head(-sin, zero)), two(head(zero, sin))


def kernel(x_prompt, x_sample, cache_k_l0, cache_v_l0, state_fwd_l1, state_bwd_l1, c, c_ctx,
           l0_norm_w, l0_mod_w, l0_mod_b, l0_w_in, l0_q_norm, l0_k_norm, l0_w_out,
           l1_norm_w, l1_mod_w, l1_mod_b, l1_w_in, l1_conv_w, l1_conv_b, l1_dt_bias_f, l1_dt_bias_b,
           l1_a_log_f, l1_a_log_b, l1_d_skip, l1_gnorm_w, l1_w_out):
    dec_b = x_sample.shape[0]
    pad = lambda u, n: jnp.pad(u, (0, n - u.shape[0]))

    n_rows = 16
    cond = jnp.zeros((n_rows, D_MODEL), f32).at[0].set(c_ctx).at[1:1 + dec_b].set(c)
    mod0 = _ada_mod(cond, l0_mod_w, l0_mod_b).reshape(n_rows, 1, 3 * D_MODEL)
    mod1 = _ada_mod(cond, l1_mod_w, l1_mod_b).reshape(n_rows, 1, 3 * D_MODEL)

    q_scale = HEAD_DIM ** -0.5 * LOG2_E
    qn = (jnp.tile(l0_q_norm, N_HEADS) * q_scale).reshape(1, ATTN_WIDTH)
    kn = jnp.tile(l0_k_norm, N_KV_HEADS).reshape(1, KV_WIDTH)
    head_of = jnp.arange(ATTN_WIDTH) // HEAD_DIM
    e = (head_of[:, None] == jnp.arange(LANES)[None, :]).astype(bf16)
    et = e.T
    w_in0 = l0_w_in.astype(bf16)
    w_out0 = l0_w_out.astype(bf16)
    w_zx = l1_w_in.astype(bf16)
    w_dt = jnp.pad(l1_w_in[:, D_INNER + CONV_DIM:], ((0, 0), (0, LANES - 2 * SSD_HEADS))).astype(bf16)
    w_out1 = l1_w_out.astype(bf16)
    dt_bias = pad(jnp.concatenate([l1_dt_bias_f, l1_dt_bias_b]), LANES).reshape(1, LANES)
    alog = pad(jnp.concatenate([l1_a_log_f, l1_a_log_b]), LANES).reshape(1, LANES)
    dskip = jnp.repeat(l1_d_skip, SSD_HEAD_DIM).reshape(1, D_INNER)
    nw0 = l0_norm_w.reshape(1, D_MODEL)
    nw1 = l1_norm_w.reshape(1, D_MODEL)
    gw = l1_gnorm_w.reshape(1, D_INNER)
    conv_b = l1_conv_b.reshape(1, CONV_DIM)
    tri = jnp.arange(CHUNK)
    lmat = (tri[None, :] <= tri[:, None]).astype(bf16)
    umat = lmat.T
    rope_tabs = _rope_tables(x_sample.shape[1])
    ssd_head_of = jnp.arange(D_INNER) // SSD_HEAD_DIM
    e_fwd = (jnp.arange(LANES)[:, None] == ssd_head_of[None, :]).astype(bf16)
    e_bwd = (jnp.arange(LANES)[:, None] == SSD_HEADS + ssd_head_of[None, :]).astype(bf16)

    kc = cache_k_l0.transpose(0, 2, 1, 3).astype(bf16)
    vc = cache_v_l0.transpose(0, 2, 3, 1).astype(bf16)
    kk_ctx = jnp.concatenate([kc, kc], axis=-1)
    vt_ctx = jnp.concatenate(
        [vc, jnp.ones(vc.shape[:2] + (VT_ROWS - HEAD_DIM, vc.shape[3]), bf16)], axis=2)
    h0_f = state_fwd_l1.reshape(dec_b, D_INNER, D_STATE)
    h0_b = state_bwd_l1.reshape(dec_b, D_INNER, D_STATE)

    outs = {}
    for latent, x in ((False, x_prompt), (True, x_sample)):
        res = _l0_inproj(x, mod0, nw0, w_in0, qn, kn, e, et, rope_tabs, latent=latent,
                         tm=4 * L0_SUB_ROWS if latent else L0_SUB_ROWS)
        if latent:
            q, kk, vt, gs = res
            kv = [kk_ctx, vt_ctx, kk, vt]
        else:
            q, kk, vt, gs, k_new, v_new = res
            kv = [kk, vt]
        o = _attention(q, kv, tq=4 * Q_SUB_ROWS if latent else Q_SUB_ROWS)
        x1, z, xs, bc, acs, rowt, w, lsum = _mid(
            x, o, gs, mod0, mod1, w_out0, nw1, w_zx, w_dt, l1_conv_w, conv_b, dt_bias, alog,
            lmat, umat, latent=latent, tm=512 if latent else 256)
        fwd = _ssd_states(xs, bc, acs, w, e_fwd, h0_f if latent else None, latent=latent)
        bwd = _ssd_main(xs, bc, acs, rowt, lsum, w, e_bwd, fwd[0], h0_b if latent else None,
                        z, x1, mod1, gw, w_out1, dskip, latent=latent)
        if latent:
            outs["y_sample"] = bwd[0]
        else:
            b = x.shape[0]
            outs["y_prompt"] = bwd[0]
            outs["k"] = k_new.reshape(b, -1, N_KV_HEADS, HEAD_DIM)
            outs["v"] = v_new.reshape(b, -1, N_KV_HEADS, HEAD_DIM)
            outs["hf"] = fwd[1].reshape(b, SSD_HEADS, SSD_HEAD_DIM, D_STATE)
            outs["hb"] = bwd[1].reshape(b, SSD_HEADS, SSD_HEAD_DIM, D_STATE)
    return (outs["y_prompt"], outs["y_sample"], outs["k"], outs["v"], outs["hf"], outs["hb"])
```

```python
import functools
import math

import jax
import jax.numpy as jnp
from jax import lax
from jax.experimental import pallas as pl
from jax.experimental.pallas import tpu as pltpu

f32 = jnp.float32
bf16 = jnp.bfloat16

D_MODEL = 1024
GRID_W = 64
EPS = 1e-6
N_HEADS = 16
N_KV_HEADS = 4
HEAD_DIM = 64
ATTN_WIDTH = N_HEADS * HEAD_DIM
KV_WIDTH = N_KV_HEADS * HEAD_DIM
AXIS_DIM = HEAD_DIM // 2
ROPE_THETA = 10000.0
D_INNER = 2048
SSD_HEAD_DIM = 64
SSD_HEADS = D_INNER // SSD_HEAD_DIM
SSD_GROUPS = 4
HEADS_PER_GROUP = SSD_HEADS // SSD_GROUPS
D_STATE = 128
CHUNK = 128
BC_WIDTH = 2 * SSD_GROUPS * D_STATE
CONV_DIM = D_INNER + BC_WIDTH

LANES = 128
BF16_ROWS = 16
HALO = BF16_ROWS
PROJ_BLOCK = 256
PROJ_LOOKAHEAD = 2
L0_SUB_ROWS = 256
SSD_STEP_CHUNKS = 4
HALF_VREG_ROWS = 8
GROUP_W = HEADS_PER_GROUP * SSD_HEAD_DIM
VT_ROWS = HEAD_DIM + BF16_ROWS
Q_SUB_ROWS = 256
KEY_CHUNK = 256
SCORE_LOOKAHEAD = 8
VMEM_LIMIT = 56 * 1024 * 1024

LOG2_E = math.log2(math.e)


def _params(n_axes):
    return pltpu.CompilerParams(
        dimension_semantics=("arbitrary",) * n_axes, vmem_limit_bytes=VMEM_LIMIT)


def _dot(a, b):
    return jnp.dot(a, b, preferred_element_type=f32)


def _dot_nt(a, b):
    return lax.dot_general(a, b, (((1,), (1,)), ((), ())), preferred_element_type=f32)


def _split(x):
    hi = x.astype(bf16)
    lo = (x - hi.astype(f32)).astype(bf16)
    return hi, lo


def _silu(x):
    return x * jax.nn.sigmoid(x)


def _softplus(x):
    return jnp.maximum(x, 0.0) + jnp.log1p(jnp.exp(-jnp.abs(x)))


def _modulated_norm(x, norm_w, shift, scale):
    ms = jnp.mean(x * x, axis=-1, keepdims=True)
    return (x * lax.rsqrt(ms + EPS) * norm_w) * (1.0 + scale) + shift


def _ada_mod_kernel(cond_ref, w_ref, b_ref, o_ref):
    s = _silu(cond_ref[...])
    sh, sl = _split(s)
    wh, wl = _split(w_ref[...])
    o_ref[...] = _dot(sh, wh) + (_dot(sh, wl) + _dot(sl, wh)) + b_ref[...]


def _ada_mod(cond, mod_w, mod_b):
    rows = cond.shape[0]
    tn = 1024
    return pl.pallas_call(
        _ada_mod_kernel,
        grid=(3 * D_MODEL // tn,),
        in_specs=[pl.BlockSpec((rows, D_MODEL), lambda j: (0, 0)),
                  pl.BlockSpec((D_MODEL, tn), lambda j: (0, j)),
                  pl.BlockSpec((1, tn), lambda j: (0, j))],
        out_specs=pl.BlockSpec((rows, tn), lambda j: (0, j)),
        out_shape=jax.ShapeDtypeStruct((rows, 3 * D_MODEL), f32),
        compiler_params=_params(1),
        name="ada_mod",
    )(cond, mod_w, mod_b.reshape(1, -1))


def _head_rms(t, e, et, w):
    hi, lo = _split(t * t)
    ss = _dot(hi, e) + _dot(lo, e)
    rh, rl = _split(lax.rsqrt(ss * (1.0 / HEAD_DIM) + EPS))
    return t * (_dot(rh, et) + _dot(rl, et)) * w


def _rope(t, cos, sin_a, sin_b):
    width = t.shape[1]
    rep = width // LANES
    tile = lambda u: jnp.concatenate([u] * rep, axis=1)
    return (t * tile(cos) + pltpu.roll(t, width - AXIS_DIM // 2, 1) * tile(sin_a)
            + pltpu.roll(t, AXIS_DIM // 2, 1) * tile(sin_b))


def _l0_inproj_kernel(*refs, latent):
    x_ref, mod_ref, nw_ref, w_ref, qn_ref, kn_ref, e_ref, et_ref = refs[:8]
    refs = refs[8:]
    if latent:
        cos_ref, sa_ref, sb_ref = refs[:3]
        q_ref, kk_ref, vt_ref, gs_ref = refs[3:]
    else:
        q_ref, kk_ref, vt_ref, gs_ref, knew_ref, vnew_ref = refs
    mod = mod_ref[0]
    e = e_ref[...]
    et = et_ref[...]
    subs = [slice(r0, r0 + L0_SUB_ROWS) for r0 in range(0, x_ref.shape[1], L0_SUB_ROWS)]
    projs = []
    for rows in subs:
        h = _modulated_norm(x_ref[0, rows, :], nw_ref[...], mod[:, :D_MODEL], mod[:, D_MODEL:2 * D_MODEL])
        projs.append(_dot(h.astype(bf16), w_ref[...]))
    low = lax.broadcasted_iota(jnp.int32, (L0_SUB_ROWS, LANES), 1) < HEAD_DIM
    ones = jnp.ones((VT_ROWS - HEAD_DIM, L0_SUB_ROWS), bf16)
    for rows, proj in zip(subs, projs):
        q = proj[:, :ATTN_WIDTH]
        k = proj[:, ATTN_WIDTH:ATTN_WIDTH + KV_WIDTH]
        v = proj[:, ATTN_WIDTH + KV_WIDTH:ATTN_WIDTH + 2 * KV_WIDTH]
        g = proj[:, ATTN_WIDTH + 2 * KV_WIDTH:]
        q = _head_rms(q, e, et, qn_ref[...])
        k = _head_rms(k, e[:KV_WIDTH], et[:, :KV_WIDTH], kn_ref[...])
        if latent:
            cos, sa, sb = cos_ref[rows, :], sa_ref[rows, :], sb_ref[rows, :]
            q = _rope(q, cos, sa, sb)
            k = _rope(k, cos, sa, sb)
        else:
            knew_ref[0, rows, :] = k
            vnew_ref[0, rows, :] = v
        q_ref[0, rows, :] = q.astype(bf16)
        gs_ref[0, rows, :] = _silu(g).astype(bf16)
        v_t = v.T
        for j in range(KV_WIDTH // LANES):
            kb = k[:, LANES * j:LANES * (j + 1)]
            kbs = pltpu.roll(kb, HEAD_DIM, 1)
            kk_ref[0, 2 * j, rows, :] = jnp.where(low, kb, kbs).astype(bf16)
            kk_ref[0, 2 * j + 1, rows, :] = jnp.where(low, kbs, kb).astype(bf16)
        for kvh in range(N_KV_HEADS):
            vt_ref[0, kvh, :HEAD_DIM, rows] = v_t[HEAD_DIM * kvh:HEAD_DIM * (kvh + 1)].astype(bf16)
            vt_ref[0, kvh, HEAD_DIM:, rows] = ones


def _l0_inproj(x, mod, norm_w, w_in, qn, kn, e, et, rope_tabs, *, latent, tm):
    b, t, _ = x.shape
    attn_in = w_in.shape[1]
    row = (lambda bi, i: (bi + 1, 0, 0)) if latent else (lambda bi, i: (0, 0, 0))
    const = lambda bi, i: (0, 0)
    in_specs = [pl.BlockSpec((1, tm, D_MODEL), lambda bi, i: (bi, i, 0)),
                pl.BlockSpec((1, 1, 3 * D_MODEL), row),
                pl.BlockSpec((1, D_MODEL), const),
                pl.BlockSpec((D_MODEL, attn_in), const),
                pl.BlockSpec((1, ATTN_WIDTH), const),
                pl.BlockSpec((1, KV_WIDTH), const),
                pl.BlockSpec((ATTN_WIDTH, LANES), const),
                pl.BlockSpec((LANES, ATTN_WIDTH), const)]
    args = [x, mod, norm_w, w_in, qn, kn, e, et]
    tok = lambda bi, i: (bi, i, 0)
    out_specs = [pl.BlockSpec((1, tm, ATTN_WIDTH), tok),
                 pl.BlockSpec((1, N_KV_HEADS, tm, LANES), lambda bi, i: (bi, 0, i, 0)),
                 pl.BlockSpec((1, N_KV_HEADS, VT_ROWS, tm), lambda bi, i: (bi, 0, 0, i)),
                 pl.BlockSpec((1, tm, ATTN_WIDTH), tok)]
    out_shape = [jax.ShapeDtypeStruct((b, t, ATTN_WIDTH), bf16),
                 jax.ShapeDtypeStruct((b, N_KV_HEADS, t, LANES), bf16),
                 jax.ShapeDtypeStruct((b, N_KV_HEADS, VT_ROWS, t), bf16),
                 jax.ShapeDtypeStruct((b, t, ATTN_WIDTH), bf16)]
    if latent:
        in_specs += [pl.BlockSpec((tm, LANES), lambda bi, i: (i, 0))] * 3
        args += list(rope_tabs)
    else:
        out_specs += [pl.BlockSpec((1, tm, KV_WIDTH), tok)] * 2
        out_shape += [jax.ShapeDtypeStruct((b, t, KV_WIDTH), f32)] * 2
    return pl.pallas_call(
        functools.partial(_l0_inproj_kernel, latent=latent),
        grid=(b, t // tm),
        in_specs=in_specs, out_specs=out_specs, out_shape=out_shape,
        compiler_params=_params(2),
        name="l0_inproj_latent" if latent else "l0_inproj_context",
    )(*args)


def _attn_kernel(*refs, n_src):
    q_ref = refs[0]
    kv_refs = refs[1:1 + 2 * n_src]
    o_ref = refs[-1]
    n_sub = q_ref.shape[1] // Q_SUB_ROWS
    low = lax.broadcasted_iota(jnp.int32, (Q_SUB_ROWS, LANES), 1) < HEAD_DIM
    zero = jnp.zeros((Q_SUB_ROWS, LANES), bf16)
    chunks = []
    for i in range(n_src):
        n_keys = kv_refs[2 * i].shape[2]
        width = min(KEY_CHUNK, n_keys)
        chunks += [(kv_refs[2 * i], kv_refs[2 * i + 1], slice(c0, c0 + width))
                   for c0 in range(0, n_keys, width)]
    steps = [(sub, head, ch) for sub in range(n_sub) for head in range(4)
             for ch in range(len(chunks))]

    def q_masked(sub, head):
        qp = q_ref[0, Q_SUB_ROWS * sub:Q_SUB_ROWS * (sub + 1),
                   LANES * (head // 2):LANES * (head // 2 + 1)]
        return jnp.where(low, qp, zero) if head % 2 == 0 else jnp.where(low, zero, qp)

    def scores(step):
        sub, head, ch = step
        k_ref, _, keys = chunks[ch]
        return _dot_nt(k_ref[0, 0, keys, :], q_masked(sub, head))

    halves = []
    m = acc = None
    queue = [scores(st) for st in steps[:SCORE_LOOKAHEAD]]
    for idx, (sub, head, ch) in enumerate(steps):
        s = queue.pop(0)
        if idx + SCORE_LOOKAHEAD < len(steps):
            queue.append(scores(steps[idx + SCORE_LOOKAHEAD]))
        _, vt_ref, keys = chunks[ch]
        s = s.astype(bf16)
        slab = s.shape[0] // 4
        m_c = jnp.maximum(jnp.maximum(s[:slab], s[slab:2 * slab]),
                          jnp.maximum(s[2 * slab:3 * slab], s[3 * slab:]))
        m_c = jnp.max(m_c.astype(f32), axis=0, keepdims=True)
        m_new = m_c if ch == 0 else jnp.maximum(m, m_c)
        pv = _dot(vt_ref[0, 0, :, keys], jnp.exp2(s - m_new.astype(bf16)))
        acc = pv if ch == 0 else acc * jnp.exp2(m - m_new) + pv
        m = m_new
        if ch == len(chunks) - 1:
            halves.append(acc[:HEAD_DIM] / acc[HEAD_DIM:HEAD_DIM + 1])
            if head % 2 == 1:
                o_ref[0, Q_SUB_ROWS * sub:Q_SUB_ROWS * (sub + 1),
                      LANES * (head // 2):LANES * (head // 2 + 1)] = (
                    jnp.concatenate(halves, axis=0).T.astype(bf16))
                halves = []


def _attention(q, kv_sources, *, tq):
    b, t, _ = q.shape
    in_specs = [pl.BlockSpec((1, tq, 2 * LANES), lambda bi, g, i: (bi, i, g))]
    args = [q]
    for arr in kv_sources:
        in_specs.append(pl.BlockSpec((1, 1) + arr.shape[2:], lambda bi, g, i: (bi, g, 0, 0)))
        args.append(arr)
    return pl.pallas_call(
        functools.partial(_attn_kernel, n_src=len(kv_sources) // 2),
        grid=(b, N_KV_HEADS, t // tq),
        in_specs=in_specs,
        out_specs=pl.BlockSpec((1, tq, 2 * LANES), lambda bi, g, i: (bi, i, g)),
        out_shape=jax.ShapeDtypeStruct((b, t, ATTN_WIDTH), bf16),
        compiler_params=_params(3),
        name="attention_latent" if len(kv_sources) > 2 else "attention_context",
    )(*args)


def _mid_kernel(x_ref, xp_ref, xn_ref, o_ref, op_ref, on_ref, g_ref, gp_ref, gn_ref,
                mod0_ref, mod1_ref, wout_ref, nw_ref, wzx_ref, wdt_ref, cw_ref, cb_ref, dtb_ref,
                alog_ref, lmat_ref, umat_ref,
                x1_ref, z_ref, xs_ref, bc_ref, acs_ref, rowt_ref, w_ref, lsum_ref):
    i = pl.program_id(1)
    last = pl.num_programs(1) - 1
    tm = x_ref.shape[1]
    rows = tm + 2 * HALO
    body = slice(HALO, HALO + tm)
    cat = lambda a, b_, c: jnp.concatenate([a[0], b_[0], c[0]], axis=0)
    gated = cat(op_ref, o_ref, on_ref) * cat(gp_ref, g_ref, gn_ref)
    attn = jnp.concatenate([_dot(gated[:rows // 2], wout_ref[...]),
                            _dot(gated[rows // 2:], wout_ref[...])], axis=0)
    x1 = cat(xp_ref, x_ref, xn_ref) + mod0_ref[0][:, 2 * D_MODEL:] * attn
    x1_ref[0] = x1[body]
    mod = mod1_ref[0]
    h = _modulated_norm(x1, nw_ref[...], mod[:, :D_MODEL], mod[:, D_MODEL:2 * D_MODEL])
    h_body = h[body].astype(bf16)
    h = jnp.concatenate([jnp.where(i == 0, 0.0, h[:HALO]).astype(bf16), h_body,
                         jnp.where(i == last, 0.0, h[HALO + tm:]).astype(bf16)], axis=0)

    n_z = D_INNER // PROJ_BLOCK
    tasks = []
    for j in range(CONV_DIM // PROJ_BLOCK):
        tasks.append(("conv", j))
        if j < n_z:
            tasks.append(("z", j))

    def project(task):
        kind, j = task
        if kind == "z":
            return _dot(h_body, wzx_ref[:, PROJ_BLOCK * j:PROJ_BLOCK * (j + 1)])
        w_blk = wzx_ref[:, D_INNER + PROJ_BLOCK * j:D_INNER + PROJ_BLOCK * (j + 1)]
        return jnp.concatenate([_dot(h[:rows // 2], w_blk), _dot(h[rows // 2:], w_blk)], axis=0)

    queue = [project(t) for t in tasks[:PROJ_LOOKAHEAD]]
    for idx, (kind, j) in enumerate(tasks):
        blk = queue.pop(0)
        if idx + PROJ_LOOKAHEAD < len(tasks):
            queue.append(project(tasks[idx + PROJ_LOOKAHEAD]))
        cols = slice(PROJ_BLOCK * j, PROJ_BLOCK * (j + 1))
        if kind == "z":
            z_ref[0, :, cols] = blk.astype(bf16)
            continue
        act = _silu(cw_ref[0:1, cols] * pltpu.roll(blk, 1, 0)[body] + cw_ref[1:2, cols] * blk[body]
                    + cw_ref[2:3, cols] * pltpu.roll(blk, rows - 1, 0)[body]
                    + cb_ref[:, cols]).astype(bf16)
        if j < n_z:
            xs_ref[0, :, cols] = act
        else:
            bc_ref[0, :, PROJ_BLOCK * (j - n_z):PROJ_BLOCK * (j - n_z + 1)] = act

    dt = _softplus(_dot(h_body, wdt_ref[...]) + dtb_ref[...])
    a = -jnp.exp(alog_ref[...])
    fwd_lane = lax.broadcasted_iota(jnp.int32, (CHUNK, LANES), 1) < SSD_HEADS
    lower, upper = lmat_ref[...], umat_ref[...]
    for c in range(tm // CHUNK):
        dtc = dt[CHUNK * c:CHUNK * (c + 1)]
        ah, al = _split(dtc * a)
        acs = jnp.where(fwd_lane, _dot(lower, ah) + _dot(lower, al), _dot(upper, ah) + _dot(upper, al))
        end_row = jnp.where(fwd_lane[0:1], acs[CHUNK - 1:CHUNK], acs[0:1])
        w_ref[0, CHUNK * c:CHUNK * (c + 1)] = dtc * jnp.exp(end_row - acs)
        acs2 = acs * LOG2_E
        acs_ref[0, CHUNK * c:CHUNK * (c + 1)] = acs2
        rowt_ref[0, c] = (acs2 - jnp.log2(dtc)).T
        lsum_ref[0, c] = jnp.log2(dtc + pltpu.roll(dtc, LANES - SSD_HEADS, 1)).T


def _mid(x, o, gs, mod0, mod1, w_out0, norm_w, w_zx, w_dt, conv_w, conv_b, dt_bias, alog, lmat, umat,
         *, latent, tm):
    b, t, _ = x.shape
    row = (lambda bi, i: (bi + 1, 0, 0)) if latent else (lambda bi, i: (0, 0, 0))
    const = lambda bi, i: (0, 0)
    tok = lambda bi, i: (bi, i, 0)
    hb = tm // HALO
    n_hb = t // HALO
    prev_map = lambda bi, i: (bi, jnp.maximum(i * hb - 1, 0), 0)
    next_map = lambda bi, i: (bi, jnp.minimum((i + 1) * hb, n_hb - 1), 0)
    halo3 = lambda width: [pl.BlockSpec((1, tm, width), tok),
                           pl.BlockSpec((1, HALO, width), prev_map),
                           pl.BlockSpec((1, HALO, width), next_map)]
    resident = lambda shape: pl.BlockSpec(shape, const, pipeline_mode=pl.Buffered(1))
    chunk4 = lambda bi, i: (bi, i, 0, 0)
    ncs = tm // CHUNK
    return pl.pallas_call(
        _mid_kernel,
        grid=(b, t // tm),
        in_specs=halo3(D_MODEL) + halo3(ATTN_WIDTH) + halo3(ATTN_WIDTH) + [
            pl.BlockSpec((1, 1, 3 * D_MODEL), row),
            pl.BlockSpec((1, 1, 3 * D_MODEL), row),
            resident((ATTN_WIDTH, D_MODEL)),
            pl.BlockSpec((1, D_MODEL), const),
            resident((D_MODEL, D_INNER + CONV_DIM)),
            resident((D_MODEL, LANES)),
            pl.BlockSpec((3, CONV_DIM), const),
            pl.BlockSpec((1, CONV_DIM), const),
            pl.BlockSpec((1, LANES), const),
            pl.BlockSpec((1, LANES), const),
            pl.BlockSpec((CHUNK, CHUNK), const),
            pl.BlockSpec((CHUNK, CHUNK), const)],
        out_specs=[pl.BlockSpec((1, tm, D_MODEL), tok),
                   pl.BlockSpec((1, tm, D_INNER), tok),
                   pl.BlockSpec((1, tm, D_INNER), tok),
                   pl.BlockSpec((1, tm, BC_WIDTH), tok),
                   pl.BlockSpec((1, tm, LANES), tok),
                   pl.BlockSpec((1, ncs, LANES, CHUNK), chunk4),
                   pl.BlockSpec((1, tm, LANES), tok),
                   pl.BlockSpec((1, ncs, LANES, CHUNK), chunk4)],
        out_shape=[jax.ShapeDtypeStruct((b, t, D_MODEL), f32),
                   jax.ShapeDtypeStruct((b, t, D_INNER), bf16),
                   jax.ShapeDtypeStruct((b, t, D_INNER), bf16),
                   jax.ShapeDtypeStruct((b, t, BC_WIDTH), bf16),
                   jax.ShapeDtypeStruct((b, t, LANES), f32),
                   jax.ShapeDtypeStruct((b, t // CHUNK, LANES, CHUNK), f32),
                   jax.ShapeDtypeStruct((b, t, LANES), f32),
                   jax.ShapeDtypeStruct((b, t // CHUNK, LANES, CHUNK), f32)],
        compiler_params=_params(2),
        name="mid_latent" if latent else "mid_context",
    )(x, x, x, o, o, o, gs, gs, gs, mod0, mod1, w_out0, norm_w, w_zx, w_dt, conv_w, conv_b,
      dt_bias, alog, lmat, umat)


def _init_state(ht_ref, h0_ref):
    if h0_ref is None:
        ht_ref[...] = jnp.zeros_like(ht_ref)
    else:
        for j in range(SSD_HEADS // 2):
            ht_ref[:, LANES * j:LANES * (j + 1)] = h0_ref[0, LANES * j:LANES * (j + 1), :].T


def _write_state(hfin_ref, ht_ref):
    for j in range(SSD_HEADS // 2):
        hfin_ref[0, LANES * j:LANES * (j + 1), :] = ht_ref[:, LANES * j:LANES * (j + 1)].T


def _lane_expand_decay(acs_row, e_ref):
    hi, lo = _split(jnp.broadcast_to(jnp.exp2(acs_row), (HALF_VREG_ROWS, LANES)))
    return (_dot(hi, e_ref[...]) + _dot(lo, e_ref[...]))[0:1]


def _weighted_x(xs_ref, w_ref, e_ref, rows, g):
    gcols = slice(GROUP_W * g, GROUP_W * (g + 1))
    return xs_ref[0, rows, gcols] * _dot(w_ref[0, rows, :].astype(bf16), e_ref[:, gcols]).astype(bf16)


def _update_state(ht_ref, bc_ref, rows, g, xw, chunk_decay):
    gcols = slice(GROUP_W * g, GROUP_W * (g + 1))
    bm = bc_ref[0, rows, D_STATE * g:D_STATE * (g + 1)]
    st = _dot(bm.astype(f32).T.astype(bf16), xw)
    ht_ref[:, gcols] = ht_ref[:, gcols] * chunk_decay[:, gcols] + st


def _ssd_states_kernel(*refs, has_h0, final_state):
    xs_ref, bc_ref, acs_ref, w_ref, e_ref = refs[:5]
    refs = refs[5:]
    h0_ref = None
    if has_h0:
        h0_ref, refs = refs[0], refs[1:]
    hprev_ref, refs = refs[0], refs[1:]
    if final_state:
        hfin_ref, refs = refs[0], refs[1:]
    ht_ref = refs[0]
    step = pl.program_id(1)

    @pl.when(step == 0)
    def _init():
        _init_state(ht_ref, h0_ref)

    n_chunks = xs_ref.shape[1] // CHUNK
    xws = {(k, g): _weighted_x(xs_ref, w_ref, e_ref, slice(CHUNK * k, CHUNK * (k + 1)), g)
           for k in range(n_chunks) for g in range(SSD_GROUPS)}
    for k in range(n_chunks):
        rows = slice(CHUNK * k, CHUNK * (k + 1))
        chunk_decay = _lane_expand_decay(acs_ref[0, CHUNK * (k + 1) - 1:CHUNK * (k + 1), :], e_ref)
        hprev_ref[0, k] = ht_ref[...].astype(bf16)
        for g in range(SSD_GROUPS):
            _update_state(ht_ref, bc_ref, rows, g, xws[k, g], chunk_decay)

    if final_state:
        @pl.when(step == pl.num_programs(1) - 1)
        def _fin():
            _write_state(hfin_ref, ht_ref)


def _ssd_main_kernel(*refs, has_h0, final_state):
    xs_ref, bc_ref, acs_ref, rowt_ref, lsum_ref, w_ref, e_ref, hprev_ref = refs[:8]
    refs = refs[8:]
    h0_ref = None
    if has_h0:
        h0_ref, refs = refs[0], refs[1:]
    z_ref, x1_ref, mod_ref, gw_ref, wout_ref, dskip_ref = refs[:6]
    out_ref, refs = refs[6], refs[7:]
    if final_state:
        hfin_ref, refs = refs[0], refs[1:]
    ht_ref, y_ref, yn_ref = refs
    step = pl.program_id(1)

    @pl.when(step == 0)
    def _init():
        _init_state(ht_ref, h0_ref)

    pairs_per_group = HEADS_PER_GROUP // 2
    n_chunks = xs_ref.shape[1] // CHUNK
    ri = lax.broadcasted_iota(jnp.int32, (CHUNK, CHUNK), 0)
    ci = lax.broadcasted_iota(jnp.int32, (CHUNK, CHUNK), 1)
    past, future = ri > ci, ri < ci
    low = lax.broadcasted_iota(jnp.int32, (CHUNK, LANES), 1) < SSD_HEAD_DIM
    chunk_order = list(range(n_chunks - 1, -1, -1))

    cbs, xws = {}, {}
    for k in chunk_order:
        rows = slice(CHUNK * k, CHUNK * (k + 1))
        for g in range(SSD_GROUPS):
            bm = bc_ref[0, rows, D_STATE * g:D_STATE * (g + 1)]
            cm = bc_ref[0, rows, D_STATE * (SSD_GROUPS + g):D_STATE * (SSD_GROUPS + g + 1)]
            cbs[k, g] = _dot_nt(cm, bm)
            xws[k, g] = _weighted_x(xs_ref, w_ref, e_ref, rows, g)

    for k in chunk_order:
        rows = slice(CHUNK * k, CHUNK * (k + 1))
        acs = acs_ref[0, rows, :]
        rowt = rowt_ref[0, k]
        lsum = lsum_ref[0, k]
        chunk_decay = _lane_expand_decay(acs[0:1, :], e_ref)
        for g in range(SSD_GROUPS):
            cm_f = bc_ref[0, rows, D_STATE * (SSD_GROUPS + g):D_STATE * (SSD_GROUPS + g + 1)].astype(f32)
            cb = cbs[k, g]
            for jp in range(pairs_per_group):
                j = g * pairs_per_group + jp
                cols = slice(LANES * j, LANES * (j + 1))
                rhs = jnp.concatenate([xs_ref[0, rows, cols], hprev_ref[0, k, :, cols],
                                       ht_ref[:, cols].astype(bf16)], axis=0)
                ys = []
                for par in range(2):
                    hf = 2 * j + par
                    hb = SSD_HEADS + hf
                    col_f = jnp.broadcast_to(acs[:, hf:hf + 1], (CHUNK, CHUNK))
                    col_b = jnp.broadcast_to(acs[:, hb:hb + 1], (CHUNK, CHUNK))
                    expo = jnp.where(past, col_f - rowt[hf:hf + 1, :],
                                     jnp.where(future, col_b - rowt[hb:hb + 1, :], lsum[hf:hf + 1, :]))
                    lhs = jnp.concatenate([(cb * jnp.exp2(expo)).astype(bf16),
                                           (cm_f * jnp.exp2(col_f)).astype(bf16),
                                           (cm_f * jnp.exp2(col_b)).astype(bf16)], axis=1)
                    ys.append(_dot(lhs, rhs))
                y_ref[:, cols] = jnp.where(low, ys[0], ys[1])
            _update_state(ht_ref, bc_ref, rows, g, xws[k, g], chunk_decay)

        y = y_ref[...] + dskip_ref[...] * xs_ref[0, rows, :].astype(f32)
        yz = y * _silu(z_ref[0, rows, :].astype(f32))
        ms = jnp.mean(yz * yz, axis=-1, keepdims=True)
        yn_ref[rows, :] = (yz * lax.rsqrt(ms + EPS) * gw_ref[...]).astype(bf16)

    gate = mod_ref[0][:, 2 * D_MODEL:]
    out_ref[0] = x1_ref[0] + gate * _dot(yn_ref[...], wout_ref[...])

    if final_state:
        @pl.when(step == pl.num_programs(1) - 1)
        def _fin():
            _write_state(hfin_ref, ht_ref)


def _ssd_states(xs, bc, acs, w, e_fwd, h0, *, latent):
    b, t, _ = xs.shape
    step_chunks = min(SSD_STEP_CHUNKS, t // CHUNK)
    rows = step_chunks * CHUNK
    blk = lambda bi, c: (bi, c, 0)
    per_b = lambda bi, c: (bi, 0, 0)
    in_specs = [pl.BlockSpec((1, rows, D_INNER), blk),
                pl.BlockSpec((1, rows, BC_WIDTH), blk),
                pl.BlockSpec((1, rows, LANES), blk),
                pl.BlockSpec((1, rows, LANES), blk),
                pl.BlockSpec((LANES, D_INNER), lambda bi, c: (0, 0))]
    args = [xs, bc, acs, w, e_fwd]
    if h0 is not None:
        in_specs.append(pl.BlockSpec((1, D_INNER, D_STATE), per_b))
        args.append(h0)
    out_specs = [pl.BlockSpec((1, step_chunks, D_STATE, D_INNER), lambda bi, c: (bi, c, 0, 0))]
    out_shape = [jax.ShapeDtypeStruct((b, t // CHUNK, D_STATE, D_INNER), bf16)]
    if not latent:
        out_specs.append(pl.BlockSpec((1, D_INNER, D_STATE), per_b))
        out_shape.append(jax.ShapeDtypeStruct((b, D_INNER, D_STATE), f32))
    return pl.pallas_call(
        functools.partial(_ssd_states_kernel, has_h0=h0 is not None, final_state=not latent),
        grid=(b, t // rows),
        in_specs=in_specs, out_specs=out_specs, out_shape=out_shape,
        scratch_shapes=[pltpu.VMEM((D_STATE, D_INNER), f32)],
        compiler_params=_params(2),
        name="ssd_states",
    )(*args)


def _ssd_main(xs, bc, acs, rowt, lsum, w, e_bwd, hprev, h0, z, x1, mod, gw, wout, dskip, *, latent):
    b, t, _ = xs.shape
    step_chunks = min(SSD_STEP_CHUNKS, t // CHUNK)
    rows = step_chunks * CHUNK
    n_steps = t // rows
    blk = lambda bi, c: (bi, n_steps - 1 - c, 0)
    blk4 = lambda bi, c: (bi, n_steps - 1 - c, 0, 0)
    const = lambda bi, c: (0, 0)
    per_b = lambda bi, c: (bi, 0, 0)
    row = (lambda bi, c: (bi + 1, 0, 0)) if latent else (lambda bi, c: (0, 0, 0))
    in_specs = [pl.BlockSpec((1, rows, D_INNER), blk),
                pl.BlockSpec((1, rows, BC_WIDTH), blk),
                pl.BlockSpec((1, rows, LANES), blk),
                pl.BlockSpec((1, step_chunks, LANES, CHUNK), blk4),
                pl.BlockSpec((1, step_chunks, LANES, CHUNK), blk4),
                pl.BlockSpec((1, rows, LANES), blk),
                pl.BlockSpec((LANES, D_INNER), const),
                pl.BlockSpec((1, step_chunks, D_STATE, D_INNER), blk4)]
    args = [xs, bc, acs, rowt, lsum, w, e_bwd, hprev]
    if h0 is not None:
        in_specs.append(pl.BlockSpec((1, D_INNER, D_STATE), per_b))
        args.append(h0)
    in_specs += [pl.BlockSpec((1, rows, D_INNER), blk),
                 pl.BlockSpec((1, rows, D_MODEL), blk),
                 pl.BlockSpec((1, 1, 3 * D_MODEL), row),
                 pl.BlockSpec((1, D_INNER), const),
                 pl.BlockSpec((D_INNER, D_MODEL), const),
                 pl.BlockSpec((1, D_INNER), const)]
    args += [z, x1, mod, gw, wout, dskip]
    out_specs = [pl.BlockSpec((1, rows, D_MODEL), blk)]
    out_shape = [jax.ShapeDtypeStruct((b, t, D_MODEL), f32)]
    if not latent:
        out_specs.append(pl.BlockSpec((1, D_INNER, D_STATE), per_b))
        out_shape.append(jax.ShapeDtypeStruct((b, D_INNER, D_STATE), f32))
    return pl.pallas_call(
        functools.partial(_ssd_main_kernel, has_h0=h0 is not None, final_state=not latent),
        grid=(b, n_steps),
        in_specs=in_specs, out_specs=out_specs, out_shape=out_shape,
        scratch_shapes=[pltpu.VMEM((D_STATE, D_INNER), f32), pltpu.VMEM((CHUNK, D_INNER), f32),
                        pltpu.VMEM((rows, D_INNER), bf16)],
        compiler_params=_params(2),
        name="ssd_main",
    )(*args)


def _rope_tables(n_tokens):
    rows = n_tokens // GRID_W
    row_ids = jnp.repeat(jnp.arange(rows), GRID_W).astype(f32)
    col_ids = jnp.tile(jnp.arange(GRID_W), rows).astype(f32)
    inv_freq = 1.0 / (ROPE_THETA ** (jnp.arange(0, AXIS_DIM, 2, dtype=f32) / AXIS_DIM))
    ang = jnp.stack([row_ids[:, None] * inv_freq, col_ids[:, None] * inv_freq], axis=1)
    cos, sin, zero = jnp.cos(ang), jnp.sin(ang), jnp.zeros_like(ang)
    head = lambda a, b_: jnp.stack([a, b_], axis=2).reshape(n_tokens, HEAD_DIM)
    two = lambda u: jnp.concatenate([u, u], axis=1)
    return two(head(cos, cos)), two(head(-sin, zero)), two(head(zero, sin))


def kernel(x_prompt, x_sample, cache_k_l0, cache_v_l0, state_fwd_l1, state_bwd_l1, c, c_ctx,
           l0_norm_w, l0_mod_w, l0_mod_b, l0_w_in, l0_q_norm, l0_k_norm, l0_w_out,
           l1_norm_w, l1_mod_w, l1_mod_b, l1_w_in, l1_conv_w, l1_conv_b, l1_dt_bias_f, l1_dt_bias_b,
           l1_a_log_f, l1_a_log_b, l1_d_skip, l1_gnorm_w, l1_w_out):
    dec_b = x_sample.shape[0]
    pad = lambda u, n: jnp.pad(u, (0, n - u.shape[0]))

    n_rows = 16
    cond = jnp.zeros((n_rows, D_MODEL), f32).at[0].set(c_ctx).at[1:1 + dec_b].set(c)
    mod0 = _ada_mod(cond, l0_mod_w, l0_mod_b).reshape(n_rows, 1, 3 * D_MODEL)
    mod1 = _ada_mod(cond, l1_mod_w, l1_mod_b).reshape(n_rows, 1, 3 * D_MODEL)

    q_scale = HEAD_DIM ** -0.5 * LOG2_E
    qn = (jnp.tile(l0_q_norm, N_HEADS) * q_scale).reshape(1, ATTN_WIDTH)
    kn = jnp.tile(l0_k_norm, N_KV_HEADS).reshape(1, KV_WIDTH)
    head_of = jnp.arange(ATTN_WIDTH) // HEAD_DIM
    e = (head_of[:, None] == jnp.arange(LANES)[None, :]).astype(bf16)
    et = e.T
    w_in0 = l0_w_in.astype(bf16)
    w_out0 = l0_w_out.astype(bf16)
    w_zx = l1_w_in.astype(bf16)
    w_dt = jnp.pad(l1_w_in[:, D_INNER + CONV_DIM:], ((0, 0), (0, LANES - 2 * SSD_HEADS))).astype(bf16)
    w_out1 = l1_w_out.astype(bf16)
    dt_bias = pad(jnp.concatenate([l1_dt_bias_f, l1_dt_bias_b]), LANES).reshape(1, LANES)
    alog = pad(jnp.concatenate([l1_a_log_f, l1_a_log_b]), LANES).reshape(1, LANES)
    dskip = jnp.repeat(l1_d_skip, SSD_HEAD_DIM).reshape(1, D_INNER)
    nw0 = l0_norm_w.reshape(1, D_MODEL)
    nw1 = l1_norm_w.reshape(1, D_MODEL)
    gw = l1_gnorm_w.reshape(1, D_INNER)
    conv_b = l1_conv_b.reshape(1, CONV_DIM)
    tri = jnp.arange(CHUNK)
    lmat = (tri[None, :] <= tri[:, None]).astype(bf16)
    umat = lmat.T
    rope_tabs = _rope_tables(x_sample.shape[1])
    ssd_head_of = jnp.arange(D_INNER) // SSD_HEAD_DIM
    e_fwd = (jnp.arange(LANES)[:, None] == ssd_head_of[None, :]).astype(bf16)
    e_bwd = (jnp.arange(LANES)[:, None] == SSD_HEADS + ssd_head_of[None, :]).astype(bf16)

    kc = cache_k_l0.transpose(0, 2, 1, 3).astype(bf16)
    vc = cache_v_l0.transpose(0, 2, 3, 1).astype(bf16)
    kk_ctx = jnp.concatenate([kc, kc], axis=-1)
    vt_ctx = jnp.concatenate(
        [vc, jnp.ones(vc.shape[:2] + (VT_ROWS - HEAD_DIM, vc.shape[3]), bf16)], axis=2)
    h0_f = state_fwd_l1.reshape(dec_b, D_INNER, D_STATE)
    h0_b = state_bwd_l1.reshape(dec_b, D_INNER, D_STATE)

    outs = {}
    for latent, x in ((False, x_prompt), (True, x_sample)):
        res = _l0_inproj(x, mod0, nw0, w_in0, qn, kn, e, et, rope_tabs, latent=latent,
                         tm=4 * L0_SUB_ROWS if latent else L0_SUB_ROWS)
        if latent:
            q, kk, vt, gs = res
            kv = [kk_ctx, vt_ctx, kk, vt]
        else:
            q, kk, vt, gs, k_new, v_new = res
            kv = [kk, vt]
        o = _attention(q, kv, tq=4 * Q_SUB_ROWS if latent else Q_SUB_ROWS)
        x1, z, xs, bc, acs, rowt, w, lsum = _mid(
            x, o, gs, mod0, mod1, w_out0, nw1, w_zx, w_dt, l1_conv_w, conv_b, dt_bias, alog,
            lmat, umat, latent=latent, tm=512 if latent else 256)
        fwd = _ssd_states(xs, bc, acs, w, e_fwd, h0_f if latent else None, latent=latent)
        bwd = _ssd_main(xs, bc, acs, rowt, lsum, w, e_bwd, fwd[0], h0_b if latent else None,
                        z, x1, mod1, gw, w_out1, dskip, latent=latent)
        if latent:
            outs["y_sample"] = bwd[0]
        else:
            b = x.shape[0]
            outs["y_prompt"] = bwd[0]
            outs["k"] = k_new.reshape(b, -1, N_KV_HEADS, HEAD_DIM)
            outs["v"] = v_new.reshape(b, -1, N_KV_HEADS, HEAD_DIM)
            outs["hf"] = fwd[1].reshape(b, SSD_HEADS, SSD_HEAD_DIM, D_STATE)
            outs["hb"] = bwd[1].reshape(b, SSD_HEADS, SSD_HEAD_DIM, D_STATE)
    return (outs["y_prompt"], outs["y_sample"], outs["k"], outs["v"], outs["hf"], outs["hb"])
```
